```python
import jax, jax.numpy as jnp
from jax import lax
import numpy as np

D_MODEL = 1024
BATCH = 16
SEQ = 4096
DEPTH = 1

HEAD_DIM = 64
N_ATTN_HEADS = 8
N_KV_HEADS = 2
GQA_GROUP = N_ATTN_HEADS // N_KV_HEADS
ATTN_WIDTH = N_ATTN_HEADS * HEAD_DIM
KV_WIDTH = N_KV_HEADS * HEAD_DIM
N_RWKV_HEADS = 8
RWKV_WIDTH = N_RWKV_HEADS * HEAD_DIM
MIX_WIDTH = ATTN_WIDTH + RWKV_WIDTH
WINDOW = 128
BLOCK = 128
DECAY_LORA = 64
ICLR_LORA = 64
GATE_LORA = 128
RWKV_SHIFT_WIDTH = 3 * RWKV_WIDTH + DECAY_LORA + ICLR_LORA + GATE_LORA
IN_WIDTH = ATTN_WIDTH + 2 * KV_WIDTH + RWKV_SHIFT_WIDTH
D_FF = 4 * D_MODEL
RMS_EPS = 1e-6
GN_EPS = 64e-5
L2_EPS = 1e-12
NEG_INF = -1e30

kernel_name = 'hybrid_swa_sink_alibi_rwkv7_block'


def rms_norm(x, g):
    xf = x.astype(jnp.float32)
    y = xf * lax.rsqrt(jnp.mean(xf * xf, axis=-1, keepdims=True) + RMS_EPS)
    return (y * g.astype(jnp.float32)).astype(x.dtype)


def alibi_slopes():
    h = jnp.arange(1, N_ATTN_HEADS + 1, dtype=jnp.float32)
    return jnp.exp2(-8.0 * h / N_ATTN_HEADS)


def sliding_window_attention(q, k, v, sinks):
    B, T = q.shape[0], q.shape[1]
    nb = T // BLOCK
    qb = q.reshape(B, nb, BLOCK, N_KV_HEADS, GQA_GROUP, HEAD_DIM)

    def band(t):
        tp = jnp.pad(t, ((0, 0), (BLOCK, 0), (0, 0), (0, 0)))
        tp = tp.reshape(B, nb + 1, BLOCK, N_KV_HEADS, HEAD_DIM)
        return jnp.concatenate([tp[:, :-1], tp[:, 1:]], axis=2)

    kb, vb = band(k), band(v)
    s = jnp.einsum('bnqkgd,bnskd->bnkgqs', qb, kb).astype(jnp.float32) * (HEAD_DIM ** -0.5)
    qi = jnp.arange(BLOCK)[:, None]
    kj = jnp.arange(2 * BLOCK)[None, :]
    dist = qi - kj + BLOCK
    bias = -alibi_slopes().reshape(N_KV_HEADS, GQA_GROUP, 1, 1) * dist.astype(jnp.float32)
    key_pos = jnp.arange(nb)[:, None] * BLOCK + jnp.arange(2 * BLOCK)[None, :] - BLOCK
    valid = ((dist >= 0) & (dist < WINDOW))[None] & (key_pos >= 0)[:, None, :]
    s = jnp.where(valid[None, :, None, None], s + bias, NEG_INF)
    sink = sinks.astype(jnp.float32).reshape(1, 1, N_KV_HEADS, GQA_GROUP, 1, 1)
    m = jnp.maximum(jnp.max(s, axis=-1, keepdims=True), sink)
    e = jnp.exp(s - m)
    p = e / (jnp.sum(e, axis=-1, keepdims=True) + jnp.exp(sink - m))
    o = jnp.einsum('bnkgqs,bnskd->bnqkgd', p.astype(vb.dtype), vb)
    return o.reshape(B, T, ATTN_WIDTH)


def rwkv7_recurrence(r, w, k, v, a, b):
    B, _, H, N = r.shape

    def step(S, inp):
        rt, wt, kt, vt, at, bt = inp
        sa = jnp.einsum('bhij,bhj->bhi', S, at)
        S = S * wt[:, :, None, :] + sa[..., None] * bt[:, :, None, :] + vt[..., None] * kt[:, :, None, :]
        return S, jnp.einsum('bhij,bhj->bhi', S, rt)

    xs = tuple(jnp.moveaxis(t, 1, 0) for t in (r, w, k, v, a, b))
    S0 = jnp.zeros((B, H, N, N), jnp.float32)
    _, y = lax.scan(step, S0, xs)
    return jnp.moveaxis(y, 0, 1)


def rwkv7_time_mix(xr, xk, xv, xw, xa, xg, w0, w2, a0, a2, g2, k_k, k_a, r_k, ln_w, ln_b):
    out_dtype = xr.dtype
    f = lambda t: t.astype(jnp.float32)
    xr, xk, xv, xw, xa, xg = f(xr), f(xk), f(xv), f(xw), f(xa), f(xg)
    B, T, _ = xr.shape
    heads = lambda t: t.reshape(B, T, N_RWKV_HEADS, HEAD_DIM)
    w_log = -jax.nn.softplus(-(f(w0) + jnp.tanh(xw) @ f(w2))) - 0.5
    decay = jnp.exp(-jnp.exp(w_log))
    a = jax.nn.sigmoid(f(a0) + xa @ f(a2))
    g = jax.nn.sigmoid(xg) @ f(g2)
    kk = heads(xk * f(k_k))
    kk = kk / jnp.maximum(jnp.sqrt(jnp.sum(kk * kk, axis=-1, keepdims=True)), L2_EPS)
    k = xk * (1.0 + (a - 1.0) * f(k_a))
    r_h, k_h, v_h, a_h = heads(xr), heads(k), heads(xv), heads(a)
    y = rwkv7_recurrence(r_h, heads(decay), k_h, v_h, -kk, kk * a_h)
    mu = jnp.mean(y, axis=-1, keepdims=True)
    var = jnp.mean(jnp.square(y - mu), axis=-1, keepdims=True)
    y = ((y - mu) * lax.rsqrt(var + GN_EPS)).reshape(B, T, RWKV_WIDTH) * f(ln_w) + f(ln_b)
    bonus = jnp.sum(r_h * k_h * f(r_k).reshape(N_RWKV_HEADS, HEAD_DIM), axis=-1, keepdims=True) * v_h
    y = (y + bonus.reshape(B, T, RWKV_WIDTH)) * g
    return y.astype(out_dtype)


def setup_inputs(seed: int = 0) -> dict:
    key = jax.random.key(seed)
    ks = jax.random.split(key, 20)
    f32 = jnp.float32
    L = DEPTH

    def nrm(k, shape, scale):
        return jax.random.normal(k, shape, f32) * scale

    return {
        'x': nrm(ks[0], (BATCH, SEQ, D_MODEL), 1.0),
        'attn_norm_g': 1.0 + nrm(ks[1], (L, D_MODEL), 0.02),
        'w_in': nrm(ks[2], (L, D_MODEL, IN_WIDTH), D_MODEL ** -0.5),
        'attn_sinks': nrm(ks[3], (L, N_ATTN_HEADS), 1.0),
        'rwkv_mu': jax.random.uniform(ks[4], (L, RWKV_SHIFT_WIDTH), f32),
        'w0': jax.random.uniform(ks[5], (L, RWKV_WIDTH), f32, -4.0, 0.0),
        'w2': nrm(ks[6], (L, DECAY_LORA, RWKV_WIDTH), 0.5 * DECAY_LORA ** -0.5),
        'a0': nrm(ks[7], (L, RWKV_WIDTH), 0.1),
        'a2': nrm(ks[8], (L, ICLR_LORA, RWKV_WIDTH), ICLR_LORA ** -0.5),
        'g2': nrm(ks[9], (L, GATE_LORA, RWKV_WIDTH), GATE_LORA ** -0.5),
        'k_k': 0.85 + nrm(ks[10], (L, RWKV_WIDTH), 0.02),
        'k_a': 1.0 + nrm(ks[11], (L, RWKV_WIDTH), 0.02),
        'r_k': nrm(ks[12], (L, RWKV_WIDTH), 0.1),
        'ln_x_w': 1.0 + nrm(ks[13], (L, RWKV_WIDTH), 0.02),
        'ln_x_b': nrm(ks[14], (L, RWKV_WIDTH), 0.02),
        'w_out': nrm(ks[15], (L, MIX_WIDTH, D_MODEL), MIX_WIDTH ** -0.5),
        'mlp_norm_g': 1.0 + nrm(ks[16], (L, D_MODEL), 0.02),
        'w_up': nrm(ks[17], (L, D_MODEL, D_FF), D_MODEL ** -0.5),
        'w_down': nrm(ks[18], (L, D_FF, D_MODEL), D_FF ** -0.5),
        'final_norm_g': 1.0 + nrm(ks[19], (D_MODEL,), 0.02),
    }


def reference(x, attn_norm_g, w_in, attn_sinks, rwkv_mu, w0, w2, a0, a2, g2, k_k, k_a, r_k,
              ln_x_w, ln_x_b, w_out, mlp_norm_g, w_up, w_down, final_norm_g):
    B, T, _ = x.shape
    c1 = ATTN_WIDTH
    c2 = c1 + KV_WIDTH
    c3 = c2 + KV_WIDTH
    r1 = RWKV_WIDTH
    r2 = 2 * RWKV_WIDTH
    r3 = 3 * RWKV_WIDTH
    r4 = r3 + DECAY_LORA
    r5 = r4 + ICLR_LORA
    for l in range(DEPTH):
        h = rms_norm(x, attn_norm_g[l])
        z = h @ w_in[l]
        q = z[..., :c1].reshape(B, T, N_ATTN_HEADS, HEAD_DIM)
        ka = z[..., c1:c2].reshape(B, T, N_KV_HEADS, HEAD_DIM)
        va = z[..., c2:c3].reshape(B, T, N_KV_HEADS, HEAD_DIM)
        attn_out = sliding_window_attention(q, ka, va, attn_sinks[l])
        zr = z[..., c3:]
        zr_prev = jnp.pad(zr, ((0, 0), (1, 0), (0, 0)))[:, :-1]
        zs = zr + (zr_prev - zr) * rwkv_mu[l]
        rwkv_out = rwkv7_time_mix(zs[..., :r1], zs[..., r1:r2], zs[..., r2:r3],
                                  zs[..., r3:r4], zs[..., r4:r5], zs[..., r5:],
                                  w0[l], w2[l], a0[l], a2[l], g2[l], k_k[l], k_a[l], r_k[l],
                                  ln_x_w[l], ln_x_b[l])
        x = x + jnp.concatenate([attn_out, rwkv_out], axis=-1) @ w_out[l]
        h = rms_norm(x, mlp_norm_g[l])
        x = x + jnp.square(jax.nn.relu(h @ w_up[l])) @ w_down[l]
    return rms_norm(x, final_norm_g)
```

```python
import functools
import math

import jax
import jax.numpy as jnp
from jax import lax
from jax.experimental import pallas as pl
from jax.experimental.pallas import tpu as pltpu

F32 = jnp.float32
BF16 = jnp.bfloat16

D_MODEL = 1024
HEAD_DIM = 64
N_ATTN_HEADS = 8
N_KV_HEADS = 2
GQA_GROUP = N_ATTN_HEADS // N_KV_HEADS
ATTN_WIDTH = N_ATTN_HEADS * HEAD_DIM
KV_WIDTH = N_KV_HEADS * HEAD_DIM
QKV_WIDTH = ATTN_WIDTH + 2 * KV_WIDTH
N_RWKV_HEADS = 8
RWKV_WIDTH = N_RWKV_HEADS * HEAD_DIM
WINDOW = 128
DECAY_LORA = 64
ICLR_LORA = 64
GATE_LORA = 128
RWKV_SHIFT_WIDTH = 3 * RWKV_WIDTH + DECAY_LORA + ICLR_LORA + GATE_LORA
IN_WIDTH = QKV_WIDTH + RWKV_SHIFT_WIDTH
D_FF = 4 * D_MODEL
RMS_EPS = 1e-6
GN_EPS = 64e-5
L2_EPS = 1e-12
NEG_INF = -1e30

LANES = 128
PAIR = 2 * HEAD_DIM
N_PAIRS = RWKV_WIDTH // PAIR
CHUNK = 64
VMEM_LIMIT = 56 * 1024 * 1024

TM_PROJ = 512
TQ_ATTN = 512
TM_RWKV = 512
TM_MLP = 512
TF_MLP = 512

NT_DIMS = (((1,), (1,)), ((), ()))
TN_DIMS = (((0,), (0,)), ((), ()))


def _dot(a, b):
    return jnp.dot(a, b, preferred_element_type=F32)


def _dot_nt(a, b):
    return lax.dot_general(a, b, NT_DIMS, preferred_element_type=F32)


def _dot_tn(a, b):
    return lax.dot_general(a, b, TN_DIMS, preferred_element_type=F32)


def _split_bf16(x, terms):
    parts = []
    rem = x
    for _ in range(terms - 1):
        p = rem.astype(BF16)
        parts.append(p)
        rem = rem - p.astype(F32)
    parts.append(rem.astype(BF16))
    return parts


def _dot_split_lhs(x, rhs_bf16, terms):
    acc = None
    for p in _split_bf16(x, terms):
        d = _dot(p, rhs_bf16)
        acc = d if acc is None else acc + d
    return acc


def _dot_split_rhs(lhs_bf16, x, terms):
    acc = None
    for p in _split_bf16(x, terms):
        d = _dot(lhs_bf16, p)
        acc = d if acc is None else acc + d
    return acc


def _sigmoid(x):
    return 1.0 / (1.0 + jnp.exp(-x))


def _rms_norm(x, g):
    ms = jnp.mean(x * x, axis=-1, keepdims=True)
    return x * lax.rsqrt(ms + RMS_EPS) * g


def _inproj_kernel(x_ref, g_ref, w_ref, qkv_ref, zr_ref):
    h = _rms_norm(x_ref[...], g_ref[...]).astype(BF16)
    qkv_ref[...] = _dot(h, w_ref[:, :QKV_WIDTH]).astype(BF16)
    zr_ref[...] = _dot(h, w_ref[:, QKV_WIDTH:]).astype(BF16)


def _inproj(x2, g, w_in):
    n = x2.shape[0]
    return pl.pallas_call(
        _inproj_kernel,
        grid=(n // TM_PROJ,),
        in_specs=[
            pl.BlockSpec((TM_PROJ, D_MODEL), lambda i: (i, 0)),
            pl.BlockSpec((1, D_MODEL), lambda i: (0, 0)),
            pl.BlockSpec((D_MODEL, IN_WIDTH), lambda i: (0, 0)),
        ],
        out_specs=[
            pl.BlockSpec((TM_PROJ, QKV_WIDTH), lambda i: (i, 0)),
            pl.BlockSpec((TM_PROJ, RWKV_SHIFT_WIDTH), lambda i: (i, 0)),
        ],
        out_shape=[
            jax.ShapeDtypeStruct((n, QKV_WIDTH), BF16),
            jax.ShapeDtypeStruct((n, RWKV_SHIFT_WIDTH), BF16),
        ],
        compiler_params=pltpu.CompilerParams(
            dimension_semantics=("arbitrary",), vmem_limit_bytes=VMEM_LIMIT),
        name="inproj",
    )(x2, g, w_in)


def _attn_kernel(sinks_ref, q_ref, kc_ref, vc_ref, kp_ref, vp_ref, o_ref, bias_ref):
    i = pl.program_id(1)

    @pl.when((pl.program_id(0) == 0) & (i == 0))
    def _():
        qi = lax.broadcasted_iota(jnp.int32, (WINDOW, 2 * WINDOW), 0)
        kj = lax.broadcasted_iota(jnp.int32, (WINDOW, 2 * WINDOW), 1)
        dist = qi - kj + WINDOW
        valid = (dist >= 0) & (dist < WINDOW)
        distf = dist.astype(F32)
        for h in range(N_ATTN_HEADS):
            slope = 2.0 ** (-8.0 * (h + 1) / N_ATTN_HEADS)
            bias_ref[h] = jnp.where(valid, -slope * distf, NEG_INF)

    lane = lax.broadcasted_iota(jnp.int32, (1, LANES), 1)
    low = lane < HEAD_DIM

    def placed(t):
        tf = t.astype(F32)
        sw = pltpu.roll(tf, HEAD_DIM, axis=1).astype(BF16)
        zero = jnp.zeros_like(t)
        return [[jnp.where(low, t, zero), jnp.where(low, zero, sw)],
                [jnp.where(low, sw, zero), jnp.where(low, zero, t)]]

    k_all = jnp.concatenate([kp_ref[...], kc_ref[...]], axis=0)
    v_all = jnp.concatenate([vp_ref[...], vc_ref[...]], axis=0)
    k_var = placed(k_all)
    v_var = placed(v_all)

    col = lax.broadcasted_iota(jnp.int32, (1, 2 * WINDOW), 1)
    pen = jnp.where(i == 0, NEG_INF, 0.0).astype(F32)
    first_pen = jnp.where(col < WINDOW, pen, 0.0)

    scale = HEAD_DIM ** -0.5
    for j in range(TQ_ATTN // WINDOW):
        r0 = j * WINDOW
        for pair in range(N_ATTN_HEADS // 2):
            qp = q_ref[r0:r0 + WINDOW, pair * PAIR:(pair + 1) * PAIR] * scale
            g = (2 * pair) // GQA_GROUP
            acc = None
            for o in range(2):
                h = 2 * pair + o
                kk = k_var[g][o][r0:r0 + 2 * WINDOW]
                vv = v_var[g][o][r0:r0 + 2 * WINDOW]
                s = _dot_nt(qp, kk) + bias_ref[h]
                if j == 0:
                    s = s + first_pen
                sink = sinks_ref[h]
                m = jnp.maximum(jnp.max(s, axis=-1, keepdims=True), sink)
                e = jnp.exp(s - m)
                den = jnp.sum(e, axis=-1, keepdims=True) + jnp.exp(sink - m)
                p = (e * (1.0 / den)).astype(BF16)
                d = _dot(p, vv)
                acc = d if acc is None else acc + d
            o_ref[r0:r0 + WINDOW, pair * PAIR:(pair + 1) * PAIR] = acc.astype(BF16)


def _attention(qkv3, sinks):
    b, t, _ = qkv3.shape
    nq = TQ_ATTN // WINDOW
    kcol = ATTN_WIDTH // KV_WIDTH
    prev = lambda bi, i: (bi, jnp.maximum(i * nq - 1, 0))
    return pl.pallas_call(
        _attn_kernel,
        grid=(b, t // TQ_ATTN),
        in_specs=[
            pl.BlockSpec(memory_space=pltpu.SMEM),
            pl.BlockSpec((None, TQ_ATTN, ATTN_WIDTH), lambda bi, i: (bi, i, 0)),
            pl.BlockSpec((None, TQ_ATTN, KV_WIDTH), lambda bi, i: (bi, i, kcol)),
            pl.BlockSpec((None, TQ_ATTN, KV_WIDTH), lambda bi, i: (bi, i, kcol + 1)),
            pl.BlockSpec((None, WINDOW, KV_WIDTH), lambda bi, i: prev(bi, i) + (kcol,)),
            pl.BlockSpec((None, WINDOW, KV_WIDTH), lambda bi, i: prev(bi, i) + (kcol + 1,)),
        ],
        out_specs=pl.BlockSpec((None, TQ_ATTN, ATTN_WIDTH), lambda bi, i: (bi, i, 0)),
        out_shape=jax.ShapeDtypeStruct((b, t, ATTN_WIDTH), BF16),
        scratch_shapes=[pltpu.VMEM((N_ATTN_HEADS, WINDOW, 2 * WINDOW), F32)],
        compiler_params=pltpu.CompilerParams(
            dimension_semantics=("arbitrary", "arbitrary"), vmem_limit_bytes=VMEM_LIMIT),
        name="attn",
    )(sinks, qkv3, qkv3, qkv3, qkv3, qkv3)


V_W0, V_A0, V_KK, V_KA, V_RK, V_LNW, V_LNB = range(7)
N_VECS = 8


def _rwkv_kernel(zr_ref, mu_ref, vecs_ref, wa_ref, g2_ref, ones_ref, o_ref, state_ref, prev_ref):
    ti = pl.program_id(1)

    @pl.when(ti == 0)
    def _():
        state_ref[...] = jnp.zeros_like(state_ref)
        prev_ref[...] = jnp.zeros_like(prev_ref)

    C = CHUNK
    row = lax.broadcasted_iota(jnp.int32, (C, 1), 0)
    lane = lax.broadcasted_iota(jnp.int32, (1, LANES), 1)
    low = lane < HEAD_DIM

    r2 = lax.broadcasted_iota(jnp.int32, (2 * C, 2 * C), 0)
    c2 = lax.broadcasted_iota(jnp.int32, (2 * C, 2 * C), 1)
    bd_strict = ((r2 // C) == (c2 // C)) & ((r2 % C) > (c2 % C))
    eye2 = (r2 == c2).astype(F32)
    r1 = lax.broadcasted_iota(jnp.int32, (C, 2 * C), 0)
    c1 = lax.broadcasted_iota(jnp.int32, (C, 2 * C), 1)
    rs_incl = r1 >= (c1 % C)
    rt = lax.broadcasted_iota(jnp.int32, (C, C), 0)
    ct = lax.broadcasted_iota(jnp.int32, (C, C), 1)
    tri = (rt >= ct).astype(BF16)

    mu = mu_ref[...]
    vec = lambda k: vecs_ref[k:k + 1, :]
    ones_bd = ones_ref[...]
    decay_scale = -math.exp(-0.5)

    def head_sum(x):
        return _dot_split_lhs(x, ones_bd, 2)

    def stack2(x):
        zero = jnp.zeros_like(x)
        return jnp.concatenate([jnp.where(low, x, zero), jnp.where(low, zero, x)], axis=0)

    def chunk_body(c, carry):
        r0 = pl.multiple_of(c * C, C)
        zc = zr_ref[pl.ds(r0, C), :].astype(F32)
        zprev = jnp.where(row == 0, prev_ref[0:1, :], pltpu.roll(zc, 1, axis=0))
        prev_ref[0:1, :] = zc[C - 1:C, :]
        zs = zc + (zprev - zc) * mu
        xr = zs[:, 0:RWKV_WIDTH]
        xk = zs[:, RWKV_WIDTH:2 * RWKV_WIDTH]
        xv = zs[:, 2 * RWKV_WIDTH:3 * RWKV_WIDTH]
        xwa = zs[:, 3 * RWKV_WIDTH:3 * RWKV_WIDTH + LANES]
        xg = zs[:, 3 * RWKV_WIDTH + LANES:]

        lora = _dot(jnp.where(low, jnp.tanh(xwa), xwa).astype(BF16), wa_ref[...])
        lw = decay_scale * _sigmoid(vec(V_W0) + lora[:, :RWKV_WIDTH])
        a = _sigmoid(vec(V_A0) + lora[:, RWKV_WIDTH:])
        g = _dot(_sigmoid(xg).astype(BF16), g2_ref[...])

        kk = xk * vec(V_KK)
        kk = kk / jnp.maximum(jnp.sqrt(head_sum(kk * kk)), L2_EPS)
        k = xk * (1.0 + (a - 1.0) * vec(V_KA))
        bonus = head_sum(xr * k * vec(V_RK)) * xv

        cum = _dot_split_rhs(tri, lw, 3)
        rho = cum[C // 2 - 1:C // 2, :]
        d1 = cum - rho
        e_pos = jnp.exp(d1)
        e_neg = jnp.exp(-d1)
        r_t = (xr * e_pos).astype(BF16)
        a_t = (-kk * jnp.exp(d1 - lw)).astype(BF16)
        b_t = (kk * a * e_neg).astype(BF16)
        k_t = (k * e_neg).astype(BF16)
        v_b = xv.astype(BF16)
        s_in = jnp.exp(rho)
        s_out = jnp.exp(cum[C - 1:C, :] - rho)

        ys = []
        for p in range(N_PAIRS):
            sl = slice(p * PAIR, (p + 1) * PAIR)
            a_s, b_s, k_s, v_s = stack2(a_t[:, sl]), stack2(b_t[:, sl]), stack2(k_t[:, sl]), stack2(v_b[:, sl])
            lg = jnp.concatenate([a_s, r_t[:, sl]], axis=0)
            pp = jnp.concatenate([b_s, k_s], axis=0)
            mf = _dot_nt(lg, pp)
            n_k = jnp.where(bd_strict, mf[:2 * C, :2 * C], 0.0)
            a_ak = jnp.where(bd_strict, mf[:2 * C, 2 * C:], 0.0)
            a_rb = jnp.where(rs_incl, mf[2 * C:, :2 * C], 0.0)
            a_rk = jnp.where(rs_incl, mf[2 * C:, 2 * C:], 0.0)
            t_m = eye2 + n_k
            pw = 1
            while 2 * pw < C:
                n_b = n_k.astype(BF16)
                n_k = _dot(n_b, n_b)
                t_m = t_m + _dot(t_m.astype(BF16), n_k.astype(BF16))
                pw *= 2
            s_p = state_ref[p] * s_in[:, sl]
            gs = _dot_nt(lg, s_p.astype(BF16))
            x_s = gs[:2 * C] + _dot(a_ak.astype(BF16), v_s)
            u_s = _dot(t_m.astype(BF16), x_s.astype(BF16))
            uv = jnp.concatenate([u_s.astype(BF16), v_s], axis=0)
            y_p = gs[2 * C:] + _dot(jnp.concatenate([a_rb, a_rk], axis=1).astype(BF16), uv)
            s_n = s_p + _dot_tn(uv, pp)
            state_ref[p] = s_n * s_out[:, sl]
            ys.append(y_p)
        y = jnp.concatenate(ys, axis=1)

        m1 = head_sum(y) * (1.0 / HEAD_DIM)
        dy = y - m1
        var = head_sum(dy * dy) * (1.0 / HEAD_DIM)
        yn = dy * lax.rsqrt(var + GN_EPS) * vec(V_LNW) + vec(V_LNB)
        o_ref[pl.ds(r0, C), :] = ((yn + bonus) * g).astype(BF16)
        return carry

    lax.fori_loop(0, TM_RWKV // C, chunk_body, 0)


def _rwkv(zr3, mu, vecs, wa_bd, g2, ones_bd):
    b, t, _ = zr3.shape
    const = lambda shape: pl.BlockSpec(shape, lambda bi, i: (0,) * len(shape))
    return pl.pallas_call(
        _rwkv_kernel,
        grid=(b, t // TM_RWKV),
        in_specs=[
            pl.BlockSpec((None, TM_RWKV, RWKV_SHIFT_WIDTH), lambda bi, i: (bi, i, 0)),
            const((1, RWKV_SHIFT_WIDTH)),
            const((N_VECS, RWKV_WIDTH)),
            const((LANES, 2 * RWKV_WIDTH)),
            const((GATE_LORA, RWKV_WIDTH)),
            const((RWKV_WIDTH, RWKV_WIDTH)),
        ],
        out_specs=pl.BlockSpec((None, TM_RWKV, RWKV_WIDTH), lambda bi, i: (bi, i, 0)),
        out_shape=jax.ShapeDtypeStruct((b, t, RWKV_WIDTH), BF16),
        scratch_shapes=[
            pltpu.VMEM((N_PAIRS, PAIR, PAIR), F32),
            pltpu.VMEM((8, RWKV_SHIFT_WIDTH), F32),
        ],
        compiler_params=pltpu.CompilerParams(
            dimension_semantics=("arbitrary", "arbitrary"), vmem_limit_bytes=VMEM_LIMIT),
        name="rwkv",
    )(zr3, mu, vecs, wa_bd, g2, ones_bd)


def _mlp_kernel(x_ref, at_ref, rw_ref, wo_ref, g1_ref, wu_ref, wd_ref, g2_ref, o_ref, act_ref):
    x1 = (x_ref[...] + _dot(at_ref[...], wo_ref[:ATTN_WIDTH, :])
          + _dot(rw_ref[...], wo_ref[ATTN_WIDTH:, :]))
    h = _rms_norm(x1, g1_ref[...]).astype(BF16)
    for f in range(D_FF // TF_MLP):
        sl = slice(f * TF_MLP, (f + 1) * TF_MLP)
        up = jnp.maximum(_dot(h, wu_ref[:, sl]), 0.0)
        act_ref[:, sl] = (up * up).astype(BF16)
    x2 = x1 + _dot(act_ref[...], wd_ref[...])
    o_ref[...] = _rms_norm(x2, g2_ref[...])


def _mlp(x2, attn, rwkv, w_out, g_mlp, w_up, w_down, g_final):
    n = x2.shape[0]
    const = lambda shape: pl.BlockSpec(shape, lambda i: (0, 0), pipeline_mode=pl.Buffered(1))
    return pl.pallas_call(
        _mlp_kernel,
        grid=(n // TM_MLP,),
        in_specs=[
            pl.BlockSpec((TM_MLP, D_MODEL), lambda i: (i, 0)),
            pl.BlockSpec((TM_MLP, ATTN_WIDTH), lambda i: (i, 0)),
            pl.BlockSpec((TM_MLP, RWKV_WIDTH), lambda i: (i, 0)),
            const((D_MODEL, D_MODEL)),
            const((1, D_MODEL)),
            const((D_MODEL, D_FF)),
            const((D_FF, D_MODEL)),
            const((1, D_MODEL)),
        ],
        out_specs=pl.BlockSpec((TM_MLP, D_MODEL), lambda i: (i, 0)),
        out_shape=jax.ShapeDtypeStruct((n, D_MODEL), F32),
        scratch_shapes=[pltpu.VMEM((TM_MLP, D_FF), BF16)],
        compiler_params=pltpu.CompilerParams(
            dimension_semantics=("arbitrary",), vmem_limit_bytes=VMEM_LIMIT),
        name="mlp",
    )(x2, attn, rwkv, w_out, g_mlp, w_up, w_down, g_final)


def _layer(x, attn_norm_g, w_in, attn_sinks, rwkv_mu, w0, w2, a0, a2, g2, k_k, k_a, r_k,
           ln_x_w, ln_x_b, w_out, mlp_norm_g, w_up, w_down):
    b, t, d = x.shape
    x2 = x.reshape(b * t, d)
    row = lambda v: v.reshape(1, -1)

    qkv, zr = _inproj(x2, row(attn_norm_g), w_in.astype(BF16))
    attn = _attention(qkv.reshape(b, t, QKV_WIDTH), attn_sinks)

    zero = jnp.zeros((DECAY_LORA, RWKV_WIDTH), F32)
    wa_bd = jnp.concatenate(
        [jnp.concatenate([w2, zero], axis=1), jnp.concatenate([zero, a2], axis=1)], axis=0).astype(BF16)
    vecs = jnp.stack([w0, a0, k_k, k_a, r_k, ln_x_w, ln_x_b, jnp.zeros_like(w0)], axis=0)
    head_id = jnp.arange(RWKV_WIDTH) // HEAD_DIM
    ones_bd = (head_id[:, None] == head_id[None, :]).astype(BF16)
    rwkv = _rwkv(zr.reshape(b, t, RWKV_SHIFT_WIDTH), row(rwkv_mu), vecs, wa_bd, g2.astype(BF16), ones_bd)

    return x2, attn.reshape(b * t, ATTN_WIDTH), rwkv.reshape(b * t, RWKV_WIDTH)


def kernel(x, attn_norm_g, w_in, attn_sinks, rwkv_mu, w0, w2, a0, a2, g2, k_k, k_a, r_k, ln_x_w, ln_x_b,
           w_out, mlp_norm_g, w_up, w_down, final_norm_g):
    assert attn_norm_g.shape[0] == 1, "one trunk layer"
    b, t, d = x.shape
    x2, attn, rwkv = _layer(x, attn_norm_g[0], w_in[0], attn_sinks[0], rwkv_mu[0], w0[0], w2[0], a0[0], a2[0],
                            g2[0], k_k[0], k_a[0], r_k[0], ln_x_w[0], ln_x_b[0], w_out[0], mlp_norm_g[0],
                            w_up[0], w_down[0])
    out = _mlp(x2, attn, rwkv, w_out[0].astype(BF16), mlp_norm_g[0].reshape(1, -1), w_up[0].astype(BF16),
               w_down[0].astype(BF16), final_norm_g.reshape(1, -1))
    return out.reshape(b, t, d)
```

```python
import math

import jax
import jax.numpy as jnp
from jax import lax
from jax.experimental import pallas as pl
from jax.experimental.pallas import tpu as pltpu

F32 = jnp.float32
BF16 = jnp.bfloat16

D_MODEL = 1024
HEAD_DIM = 64
N_ATTN_HEADS = 8
N_KV_HEADS = 2
GQA_GROUP = N_ATTN_HEADS // N_KV_HEADS
ATTN_WIDTH = N_ATTN_HEADS * HEAD_DIM
KV_WIDTH = N_KV_HEADS * HEAD_DIM
QKV_WIDTH = ATTN_WIDTH + 2 * KV_WIDTH
N_RWKV_HEADS = 8
RWKV_WIDTH = N_RWKV_HEADS * HEAD_DIM
WINDOW = 128
DECAY_LORA = 64
ICLR_LORA = 64
GATE_LORA = 128
RWKV_SHIFT_WIDTH = 3 * RWKV_WIDTH + DECAY_LORA + ICLR_LORA + GATE_LORA
IN_WIDTH = QKV_WIDTH + RWKV_SHIFT_WIDTH
D_FF = 4 * D_MODEL
RMS_EPS = 1e-6
GN_EPS = 64e-5
L2_EPS = 1e-12
NEG_INF = -1e30

LANES = 128
PAIR = 2 * HEAD_DIM
N_PAIRS = RWKV_WIDTH // PAIR
CHUNK = 64
VMEM_LIMIT = 56 * 1024 * 1024

TM_PROJ = 512
TQ_ATTN = 512
TM_RWKV = 512
TM_MLP = 512
TF_MLP = 512

NT_DIMS = (((1,), (1,)), ((), ()))
TN_DIMS = (((0,), (0,)), ((), ()))


def _dot(a, b):
    return jnp.dot(a, b, preferred_element_type=F32)


def _dot_nt(a, b):
    return lax.dot_general(a, b, NT_DIMS, preferred_element_type=F32)


def _dot_tn(a, b):
    return lax.dot_general(a, b, TN_DIMS, preferred_element_type=F32)


def _split_bf16(x, terms):
    parts = []
    rem = x
    for _ in range(terms - 1):
        p = rem.astype(BF16)
        parts.append(p)
        rem = rem - p.astype(F32)
    parts.append(rem.astype(BF16))
    return parts


def _dot_split_lhs(x, rhs_bf16, terms):
    acc = None
    for p in _split_bf16(x, terms):
        d = _dot(p, rhs_bf16)
        acc = d if acc is None else acc + d
    return acc


def _dot_split_rhs(lhs_bf16, x, terms):
    acc = None
    for p in _split_bf16(x, terms):
        d = _dot(lhs_bf16, p)
        acc = d if acc is None else acc + d
    return acc


def _sigmoid(x):
    return 1.0 / (1.0 + jnp.exp(-x))


def _rms_norm(x, g):
    ms = jnp.mean(x * x, axis=-1, keepdims=True)
    return x * lax.rsqrt(ms + RMS_EPS) * g


def _inproj_kernel(x_ref, g_ref, w_ref, qkv_ref, zr_ref):
    h = _rms_norm(x_ref[...], g_ref[...]).astype(BF16)
    qkv_ref[...] = _dot(h, w_ref[:, :QKV_WIDTH]).astype(BF16)
    zr_ref[...] = _dot(h, w_ref[:, QKV_WIDTH:]).astype(BF16)


def _inproj(x2, g, w_in):
    n = x2.shape[0]
    return pl.pallas_call(
        _inproj_kernel,
        grid=(n // TM_PROJ,),
        in_specs=[
            pl.BlockSpec((TM_PROJ, D_MODEL), lambda i: (i, 0)),
            pl.BlockSpec((1, D_MODEL), lambda i: (0, 0)),
            pl.BlockSpec((D_MODEL, IN_WIDTH), lambda i: (0, 0)),
        ],
        out_specs=[
            pl.BlockSpec((TM_PROJ, QKV_WIDTH), lambda i: (i, 0)),
            pl.BlockSpec((TM_PROJ, RWKV_SHIFT_WIDTH), lambda i: (i, 0)),
        ],
        out_shape=[
            jax.ShapeDtypeStruct((n, QKV_WIDTH), BF16),
            jax.ShapeDtypeStruct((n, RWKV_SHIFT_WIDTH), BF16),
        ],
        compiler_params=pltpu.CompilerParams(
            dimension_semantics=("arbitrary",), vmem_limit_bytes=VMEM_LIMIT),
        name="inproj",
    )(x2, g, w_in)


def _attn_kernel(sinks_ref, q_ref, kc_ref, vc_ref, kp_ref, vp_ref, o_ref, bias_ref):
    i = pl.program_id(1)

    @pl.when((pl.program_id(0) == 0) & (i == 0))
    def _():
        qi = lax.broadcasted_iota(jnp.int32, (WINDOW, 2 * WINDOW), 0)
        kj = lax.broadcasted_iota(jnp.int32, (WINDOW, 2 * WINDOW), 1)
        dist = qi - kj + WINDOW
        valid = (dist >= 0) & (dist < WINDOW)
        distf = dist.astype(F32)
        for h in range(N_ATTN_HEADS):
            slope = 2.0 ** (-8.0 * (h + 1) / N_ATTN_HEADS)
            bias_ref[h] = jnp.where(valid, -slope * distf, NEG_INF)

    lane = lax.broadcasted_iota(jnp.int32, (1, LANES), 1)
    low = lane < HEAD_DIM

    def placed(t):
        tf = t.astype(F32)
        sw = pltpu.roll(tf, HEAD_DIM, axis=1).astype(BF16)
        zero = jnp.zeros_like(t)
        return [[jnp.where(low, t, zero), jnp.where(low, zero, sw)],
                [jnp.where(low, sw, zero), jnp.where(low, zero, t)]]

    k_all = jnp.concatenate([kp_ref[...], kc_ref[...]], axis=0)
    v_all = jnp.concatenate([vp_ref[...], vc_ref[...]], axis=0)
    k_var = placed(k_all)
    v_var = placed(v_all)

    col = lax.broadcasted_iota(jnp.int32, (1, 2 * WINDOW), 1)
    pen = jnp.where(i == 0, NEG_INF, 0.0).astype(F32)
    first_pen = jnp.where(col < WINDOW, pen, 0.0)

    scale = HEAD_DIM ** -0.5
    for j in range(TQ_ATTN // WINDOW):
        r0 = j * WINDOW
        for pair in range(N_ATTN_HEADS // 2):
            qp = q_ref[r0:r0 + WINDOW, pair * PAIR:(pair + 1) * PAIR] * scale
            g = (2 * pair) // GQA_GROUP
            acc = None
            for o in range(2):
                h = 2 * pair + o
                kk = k_var[g][o][r0:r0 + 2 * WINDOW]
                vv = v_var[g][o][r0:r0 + 2 * WINDOW]
                s = _dot_nt(qp, kk) + bias_ref[h]
                if j == 0:
                    s = s + first_pen
                sink = sinks_ref[h]
                m = jnp.maximum(jnp.max(s, axis=-1, keepdims=True), sink)
                e = jnp.exp(s - m)
                den = jnp.sum(e, axis=-1, keepdims=True) + jnp.exp(sink - m)
                p = (e * (1.0 / den)).astype(BF16)
                d = _dot(p, vv)
                acc = d if acc is None else acc + d
            o_ref[r0:r0 + WINDOW, pair * PAIR:(pair + 1) * PAIR] = acc.astype(BF16)


def _attention(qkv3, sinks):
    b, t, _ = qkv3.shape
    nq = TQ_ATTN // WINDOW
    kcol = ATTN_WIDTH // KV_WIDTH
    prev = lambda bi, i: (bi, jnp.maximum(i * nq - 1, 0))
    return pl.pallas_call(
        _attn_kernel,
        grid=(b, t // TQ_ATTN),
        in_specs=[
            pl.BlockSpec(memory_space=pltpu.SMEM),
            pl.BlockSpec((None, TQ_ATTN, ATTN_WIDTH), lambda bi, i: (bi, i, 0)),
            pl.BlockSpec((None, TQ_ATTN, KV_WIDTH), lambda bi, i: (bi, i, kcol)),
            pl.BlockSpec((None, TQ_ATTN, KV_WIDTH), lambda bi, i: (bi, i, kcol + 1)),
            pl.BlockSpec((None, WINDOW, KV_WIDTH), lambda bi, i: prev(bi, i) + (kcol,)),
            pl.BlockSpec((None, WINDOW, KV_WIDTH), lambda bi, i: prev(bi, i) + (kcol + 1,)),
        ],
        out_specs=pl.BlockSpec((None, TQ_ATTN, ATTN_WIDTH), lambda bi, i: (bi, i, 0)),
        out_shape=jax.ShapeDtypeStruct((b, t, ATTN_WIDTH), BF16),
        scratch_shapes=[pltpu.VMEM((N_ATTN_HEADS, WINDOW, 2 * WINDOW), F32)],
        compiler_params=pltpu.CompilerParams(
            dimension_semantics=("arbitrary", "arbitrary"), vmem_limit_bytes=VMEM_LIMIT),
        name="attn",
    )(sinks, qkv3, qkv3, qkv3, qkv3, qkv3)


V_W0, V_A0, V_KK, V_KA, V_RK, V_LNW, V_LNB = range(7)
N_VECS = 8
GROUP = 2 * CHUNK
N_CHUNKS = TM_RWKV // CHUNK


def _rwkv_kernel(zr_ref, mu_ref, vecs_ref, wa_ref, g2_ref, ones_ref, o_ref,
                 state_ref, prev_ref, phi_ref, delta_ref, rw_ref, yc_ref, sin_ref, sout_ref,
                 bonus_ref, gate_ref, y_ref):
    ti = pl.program_id(1)

    @pl.when(ti == 0)
    def _():
        state_ref[...] = jnp.zeros_like(state_ref)
        prev_ref[...] = jnp.zeros_like(prev_ref)

    C = CHUNK
    G = GROUP
    lane = lax.broadcasted_iota(jnp.int32, (1, LANES), 1)
    low = lane < HEAD_DIM
    rowg = lax.broadcasted_iota(jnp.int32, (G, 1), 0)

    r2 = lax.broadcasted_iota(jnp.int32, (2 * C, 2 * C), 0)
    c2 = lax.broadcasted_iota(jnp.int32, (2 * C, 2 * C), 1)
    same_blk = (r2 // C) == (c2 // C)
    bd_strict = same_blk & ((r2 % C) > (c2 % C))
    eye2 = (r2 == c2).astype(F32)
    r1 = lax.broadcasted_iota(jnp.int32, (C, 2 * C), 0)
    c1 = lax.broadcasted_iota(jnp.int32, (C, 2 * C), 1)
    rs_incl = r1 >= (c1 % C)
    mid = C // 2 - 1
    emat = jnp.where(same_blk, (c2 <= r2).astype(F32) - ((c2 % C) <= mid).astype(F32), 0.0).astype(BF16)

    mu = mu_ref[...]
    vec = lambda k: vecs_ref[k:k + 1, :]
    ones_bd = ones_ref[...]
    decay_scale = -math.exp(-0.5)

    def head_sum(x):
        return _dot_split_lhs(x, ones_bd, 2)

    def stack2(x):
        zero = jnp.zeros_like(x)
        return jnp.concatenate([jnp.where(low, x, zero), jnp.where(low, zero, x)], axis=0)

    def group_body(gi, carry):
        r0 = pl.multiple_of(gi * G, G)
        zc = zr_ref[pl.ds(r0, G), :].astype(F32)
        zprev = jnp.where(rowg == 0, prev_ref[0:1, :], pltpu.roll(zc, 1, axis=0))
        prev_ref[0:1, :] = zc[G - 1:G, :]
        zs = zc + (zprev - zc) * mu
        xr = zs[:, 0:RWKV_WIDTH]
        xk = zs[:, RWKV_WIDTH:2 * RWKV_WIDTH]
        xv = zs[:, 2 * RWKV_WIDTH:3 * RWKV_WIDTH]
        xwa = zs[:, 3 * RWKV_WIDTH:3 * RWKV_WIDTH + LANES]
        xg = zs[:, 3 * RWKV_WIDTH + LANES:]

        lora = _dot(jnp.where(low, jnp.tanh(xwa), xwa).astype(BF16), wa_ref[...])
        lw = decay_scale * _sigmoid(vec(V_W0) + lora[:, :RWKV_WIDTH])
        a = _sigmoid(vec(V_A0) + lora[:, RWKV_WIDTH:])
        gate_ref[pl.ds(r0, G), :] = _dot(_sigmoid(xg).astype(BF16), g2_ref[...])

        kk = xk * vec(V_KK)
        kk = kk / jnp.maximum(jnp.sqrt(head_sum(kk * kk)), L2_EPS)
        k = xk * (1.0 + (a - 1.0) * vec(V_KA))
        bonus_ref[pl.ds(r0, G), :] = head_sum(xr * k * vec(V_RK)) * xv

        d1 = _dot_split_rhs(emat, lw, 3)
        e_neg = jnp.exp(-d1)
        r_t = (xr * jnp.exp(d1)).astype(BF16)
        a_t = (-kk * jnp.exp(d1 - lw)).astype(BF16)
        b_t = (kk * a * e_neg).astype(BF16)
        k_t = (k * e_neg).astype(BF16)
        v_b = xv.astype(BF16)

        chains = []
        for j in range(G // C):
            ci = gi * (G // C) + j
            rows = slice(j * C, (j + 1) * C)
            rho = lw[j * C:j * C + 1, :] - d1[j * C:j * C + 1, :]
            sin_ref[ci] = jnp.exp(rho)
            sout_ref[ci] = jnp.exp(d1[(j + 1) * C - 1:(j + 1) * C, :])
            for p in range(N_PAIRS):
                sl = slice(p * PAIR, (p + 1) * PAIR)
                a_s, b_s = stack2(a_t[rows, sl]), stack2(b_t[rows, sl])
                k_s, v_s = stack2(k_t[rows, sl]), stack2(v_b[rows, sl])
                chains.append(dict(ci=ci, j=j, sl=sl, p=p, a_s=a_s, b_s=b_s, v_s=v_s, r=r_t[rows, sl],
                                   lg=jnp.concatenate([a_s, r_t[rows, sl]], axis=0),
                                   pp=jnp.concatenate([b_s, k_s], axis=0)))

        for ch in chains:
            mf = _dot_nt(ch["lg"], ch["pp"])
            ch["n"] = jnp.where(bd_strict, mf[:2 * C, :2 * C], 0.0)
            ch["a_ak"] = jnp.where(bd_strict, mf[:2 * C, 2 * C:], 0.0).astype(BF16)
            ch["a_rb"] = jnp.where(rs_incl, mf[2 * C:, :2 * C], 0.0).astype(BF16)
            ch["a_rk"] = jnp.where(rs_incl, mf[2 * C:, 2 * C:], 0.0).astype(BF16)
            ch["t"] = eye2 + ch["n"]
        pw = 1
        while 2 * pw < C:
            for ch in chains:
                n_b = ch["n"].astype(BF16)
                ch["n"] = _dot(n_b, n_b)
            for ch in chains:
                ch["t"] = ch["t"] + _dot(ch["t"].astype(BF16), ch["n"].astype(BF16))
            pw *= 2
        for ch in chains:
            ch["av"] = _dot(ch["a_ak"], ch["v_s"]).astype(BF16)
        for ch in chains:
            wu = _dot(ch["t"].astype(BF16), jnp.concatenate([ch["a_s"], ch["av"]], axis=1))
            ch["w"] = wu[:, :PAIR].astype(BF16)
            ch["uv0"] = jnp.concatenate([wu[:, PAIR:].astype(BF16), ch["v_s"]], axis=0)
        for ch in chains:
            rows = pl.ds(r0 + ch["j"] * C, C)
            phi_ref[ch["ci"], ch["p"]] = _dot_tn(ch["w"], ch["b_s"]).astype(BF16)
            delta_ref[ch["ci"], ch["p"]] = _dot_tn(ch["uv0"], ch["pp"])
            rw_ref[rows, ch["sl"]] = (ch["r"].astype(F32) + _dot(ch["a_rb"], ch["w"])).astype(BF16)
            yc_ref[rows, ch["sl"]] = _dot(jnp.concatenate([ch["a_rb"], ch["a_rk"]], axis=1), ch["uv0"])
        return carry

    lax.fori_loop(0, TM_RWKV // G, group_body, 0)

    for c in range(N_CHUNKS):
        rows = slice(c * C, (c + 1) * C)
        s_in = sin_ref[c]
        s_out = sout_ref[c]
        for p in range(N_PAIRS):
            sl = slice(p * PAIR, (p + 1) * PAIR)
            s_p = state_ref[p] * s_in[:, sl]
            s_b = s_p.astype(BF16)
            y_ref[rows, sl] = _dot_nt(rw_ref[rows, sl], s_b) + yc_ref[rows, sl]
            state_ref[p] = (s_p + _dot(s_b, phi_ref[c, p]) + delta_ref[c, p]) * s_out[:, sl]
        if (c + 1) % (G // C) == 0:
            blk = slice((c + 1) * C - G, (c + 1) * C)
            y = y_ref[blk, :]
            m1 = head_sum(y) * (1.0 / HEAD_DIM)
            dy = y - m1
            var = head_sum(dy * dy) * (1.0 / HEAD_DIM)
            yn = dy * lax.rsqrt(var + GN_EPS) * vec(V_LNW) + vec(V_LNB)
            o_ref[blk, :] = ((yn + bonus_ref[blk, :]) * gate_ref[blk, :]).astype(BF16)


def _rwkv(zr3, mu, vecs, wa_bd, g2, ones_bd):
    b, t, _ = zr3.shape
    const = lambda shape: pl.BlockSpec(shape, lambda bi, i: (0,) * len(shape))
    return pl.pallas_call(
        _rwkv_kernel,
        grid=(b, t // TM_RWKV),
        in_specs=[
            pl.BlockSpec((None, TM_RWKV, RWKV_SHIFT_WIDTH), lambda bi, i: (bi, i, 0)),
            const((1, RWKV_SHIFT_WIDTH)),
            const((N_VECS, RWKV_WIDTH)),
            const((LANES, 2 * RWKV_WIDTH)),
            const((GATE_LORA, RWKV_WIDTH)),
            const((RWKV_WIDTH, RWKV_WIDTH)),
        ],
        out_specs=pl.BlockSpec((None, TM_RWKV, RWKV_WIDTH), lambda bi, i: (bi, i, 0)),
        out_shape=jax.ShapeDtypeStruct((b, t, RWKV_WIDTH), BF16),
        scratch_shapes=[
            pltpu.VMEM((N_PAIRS, PAIR, PAIR), F32),
            pltpu.VMEM((8, RWKV_SHIFT_WIDTH), F32),
            pltpu.VMEM((N_CHUNKS, N_PAIRS, PAIR, PAIR), BF16),
            pltpu.VMEM((N_CHUNKS, N_PAIRS, PAIR, PAIR), F32),
            pltpu.VMEM((TM_RWKV, RWKV_WIDTH), BF16),
            pltpu.VMEM((TM_RWKV, RWKV_WIDTH), F32),
            pltpu.VMEM((N_CHUNKS, 1, RWKV_WIDTH), F32),
            pltpu.VMEM((N_CHUNKS, 1, RWKV_WIDTH), F32),
            pltpu.VMEM((TM_RWKV, RWKV_WIDTH), F32),
            pltpu.VMEM((TM_RWKV, RWKV_WIDTH), F32),
            pltpu.VMEM((TM_RWKV, RWKV_WIDTH), F32),
        ],
        compiler_params=pltpu.CompilerParams(
            dimension_semantics=("arbitrary", "arbitrary"), vmem_limit_bytes=VMEM_LIMIT),
        name="rwkv",
    )(zr3, mu, vecs, wa_bd, g2, ones_bd)


def _mlp_kernel(x_ref, at_ref, rw_ref, wo_ref, g1_ref, wu_ref, wd_ref, g2_ref, o_ref, act_ref):
    x1 = (x_ref[...] + _dot(at_ref[...], wo_ref[:ATTN_WIDTH, :])
          + _dot(rw_ref[...], wo_ref[ATTN_WIDTH:, :]))
    h = _rms_norm(x1, g1_ref[...]).astype(BF16)
    for f in range(D_FF // TF_MLP):
        sl = slice(f * TF_MLP, (f + 1) * TF_MLP)
        up = jnp.maximum(_dot(h, wu_ref[:, sl]), 0.0)
        act_ref[:, sl] = (up * up).astype(BF16)
    x2 = x1 + _dot(act_ref[...], wd_ref[...])
    o_ref[...] = _rms_norm(x2, g2_ref[...])


def _mlp(x2, attn, rwkv, w_out, g_mlp, w_up, w_down, g_final):
    n = x2.shape[0]
    const = lambda shape: pl.BlockSpec(shape, lambda i: (0, 0), pipeline_mode=pl.Buffered(1))
    return pl.pallas_call(
        _mlp_kernel,
        grid=(n // TM_MLP,),
        in_specs=[
            pl.BlockSpec((TM_MLP, D_MODEL), lambda i: (i, 0)),
            pl.BlockSpec((TM_MLP, ATTN_WIDTH), lambda i: (i, 0)),
            pl.BlockSpec((TM_MLP, RWKV_WIDTH), lambda i: (i, 0)),
            const((D_MODEL, D_MODEL)),
            const((1, D_MODEL)),
            const((D_MODEL, D_FF)),
            const((D_FF, D_MODEL)),
            const((1, D_MODEL)),
        ],
        out_specs=pl.BlockSpec((TM_MLP, D_MODEL), lambda i: (i, 0)),
        out_shape=jax.ShapeDtypeStruct((n, D_MODEL), F32),
        scratch_shapes=[pltpu.VMEM((TM_MLP, D_FF), BF16)],
        compiler_params=pltpu.CompilerParams(
            dimension_semantics=("arbitrary",), vmem_limit_bytes=VMEM_LIMIT),
        name="mlp",
    )(x2, attn, rwkv, w_out, g_mlp, w_up, w_down, g_final)


def _layer(x, attn_norm_g, w_in, attn_sinks, rwkv_mu, w0, w2, a0, a2, g2, k_k, k_a, r_k,
           ln_x_w, ln_x_b, w_out, mlp_norm_g, w_up, w_down):
    b, t, d = x.shape
    x2 = x.reshape(b * t, d)
    row = lambda v: v.reshape(1, -1)

    qkv, zr = _inproj(x2, row(attn_norm_g), w_in.astype(BF16))
    attn = _attention(qkv.reshape(b, t, QKV_WIDTH), attn_sinks)

    zero = jnp.zeros((DECAY_LORA, RWKV_WIDTH), F32)
    wa_bd = jnp.concatenate(
        [jnp.concatenate([w2, zero], axis=1), jnp.concatenate([zero, a2], axis=1)], axis=0).astype(BF16)
    vecs = jnp.stack([w0, a0, k_k, k_a, r_k, ln_x_w, ln_x_b, jnp.zeros_like(w0)], axis=0)
    head_id = jnp.arange(RWKV_WIDTH) // HEAD_DIM
    ones_bd = (head_id[:, None] == head_id[None, :]).astype(BF16)
    rwkv = _rwkv(zr.reshape(b, t, RWKV_SHIFT_WIDTH), row(rwkv_mu), vecs, wa_bd, g2.astype(BF16), ones_bd)

    return x2, attn.reshape(b * t, ATTN_WIDTH), rwkv.reshape(b * t, RWKV_WIDTH)


def kernel(x, attn_norm_g, w_in, attn_sinks, rwkv_mu, w0, w2, a0, a2, g2, k_k, k_a, r_k, ln_x_w, ln_x_b,
           w_out, mlp_norm_g, w_up, w_down, final_norm_g):
    assert attn_norm_g.shape[0] == 1, "one trunk layer"
    b, t, d = x.shape
    x2, attn, rwkv = _layer(x, attn_norm_g[0], w_in[0], attn_sinks[0], rwkv_mu[0], w0[0], w2[0], a0[0], a2[0],
                            g2[0], k_k[0], k_a[0], r_k[0], ln_x_w[0], ln_x_b[0], w_out[0], mlp_norm_g[0],
                            w_up[0], w_down[0])
    out = _mlp(x2, attn, rwkv, w_out[0].astype(BF16), mlp_norm_g[0].reshape(1, -1), w_up[0].astype(BF16),
               w_down[0].astype(BF16), final_norm_g.reshape(1, -1))
    return out.reshape(b, t, d)
```

```python
import math

import jax
import jax.numpy as jnp
from jax import lax
from jax.experimental import pallas as pl
from jax.experimental.pallas import tpu as pltpu

F32 = jnp.float32
BF16 = jnp.bfloat16

D_MODEL = 1024
HEAD_DIM = 64
N_ATTN_HEADS = 8
N_KV_HEADS = 2
GQA_GROUP = N_ATTN_HEADS // N_KV_HEADS
ATTN_WIDTH = N_ATTN_HEADS * HEAD_DIM
KV_WIDTH = N_KV_HEADS * HEAD_DIM
QKV_WIDTH = ATTN_WIDTH + 2 * KV_WIDTH
N_RWKV_HEADS = 8
RWKV_WIDTH = N_RWKV_HEADS * HEAD_DIM
WINDOW = 128
DECAY_LORA = 64
ICLR_LORA = 64
GATE_LORA = 128
RWKV_SHIFT_WIDTH = 3 * RWKV_WIDTH + DECAY_LORA + ICLR_LORA + GATE_LORA
IN_WIDTH = QKV_WIDTH + RWKV_SHIFT_WIDTH
D_FF = 4 * D_MODEL
RMS_EPS = 1e-6
GN_EPS = 64e-5
L2_EPS = 1e-12
NEG_INF = -1e30
LOG2E = math.log2(math.e)
Q_SCALE = HEAD_DIM ** -0.5 * LOG2E

LANES = 128
MXU_WIDTH = 256
PAIR = 2 * HEAD_DIM
N_PAIRS = RWKV_WIDTH // PAIR
CHUNK = 64
VMEM_LIMIT = 56 * 1024 * 1024

TM_PROJ = 512
TQ_ATTN = 512
TM_RWKV = 512
TM_MLP = 512
TF_MLP = 512

NT_DIMS = (((1,), (1,)), ((), ()))
TN_DIMS = (((0,), (0,)), ((), ()))


def _dot(a, b):
    return jnp.dot(a, b, preferred_element_type=F32)


def _dot_nt(a, b):
    return lax.dot_general(a, b, NT_DIMS, preferred_element_type=F32)


def _dot_tn(a, b):
    return lax.dot_general(a, b, TN_DIMS, preferred_element_type=F32)


def _split_bf16(x, terms):
    parts = []
    rem = x
    for _ in range(terms - 1):
        p = rem.astype(BF16)
        parts.append(p)
        rem = rem - p.astype(F32)
    parts.append(rem.astype(BF16))
    return parts


def _dot_split_lhs(x, rhs_bf16, terms):
    acc = None
    for p in _split_bf16(x, terms):
        d = _dot(p, rhs_bf16)
        acc = d if acc is None else acc + d
    return acc


def _dot_split_rhs(lhs_bf16, x, terms):
    acc = None
    for p in _split_bf16(x, terms):
        d = _dot(lhs_bf16, p)
        acc = d if acc is None else acc + d
    return acc


def _sigmoid(x):
    return 1.0 / (1.0 + jnp.exp(-x))


def _rms_norm(x, g):
    ms = jnp.mean(x * x, axis=-1, keepdims=True)
    return x * lax.rsqrt(ms + RMS_EPS) * g


def _inproj_kernel(x_ref, g_ref, w_ref, qkv_ref, zr_ref):
    h = _rms_norm(x_ref[...], g_ref[...]).astype(BF16)
    qkv_ref[:, :ATTN_WIDTH] = (_dot(h, w_ref[:, :ATTN_WIDTH]) * Q_SCALE).astype(BF16)
    qkv_ref[:, ATTN_WIDTH:] = _dot(h, w_ref[:, ATTN_WIDTH:QKV_WIDTH]).astype(BF16)
    zr_ref[...] = _dot(h, w_ref[:, QKV_WIDTH:]).astype(BF16)


def _inproj(x2, g, w_in):
    n = x2.shape[0]
    return pl.pallas_call(
        _inproj_kernel,
        grid=(n // TM_PROJ,),
        in_specs=[
            pl.BlockSpec((TM_PROJ, D_MODEL), lambda i: (i, 0)),
            pl.BlockSpec((1, D_MODEL), lambda i: (0, 0)),
            pl.BlockSpec((D_MODEL, IN_WIDTH), lambda i: (0, 0)),
        ],
        out_specs=[
            pl.BlockSpec((TM_PROJ, QKV_WIDTH), lambda i: (i, 0)),
            pl.BlockSpec((TM_PROJ, RWKV_SHIFT_WIDTH), lambda i: (i, 0)),
        ],
        out_shape=[
            jax.ShapeDtypeStruct((n, QKV_WIDTH), BF16),
            jax.ShapeDtypeStruct((n, RWKV_SHIFT_WIDTH), BF16),
        ],
        compiler_params=pltpu.CompilerParams(
            dimension_semantics=("arbitrary",), vmem_limit_bytes=VMEM_LIMIT),
        name="inproj",
    )(x2, g, w_in)


def _attn_kernel(sinks_ref, q_ref, kc_ref, vc_ref, kp_ref, vp_ref, o_ref, bias_ref, kvar_ref, vvar_ref):
    i = pl.program_id(1)

    @pl.when((pl.program_id(0) == 0) & (i == 0))
    def _():
        qi = lax.broadcasted_iota(jnp.int32, (WINDOW, 2 * WINDOW), 0)
        kj = lax.broadcasted_iota(jnp.int32, (WINDOW, 2 * WINDOW), 1)
        dist = qi - kj + WINDOW
        valid = (dist >= 0) & (dist < WINDOW)
        distf = dist.astype(F32)
        for h in range(N_ATTN_HEADS):
            slope = 2.0 ** (-8.0 * (h + 1) / N_ATTN_HEADS)
            bias_ref[h] = jnp.where(valid, (-slope * LOG2E) * distf, NEG_INF)

    lane = lax.broadcasted_iota(jnp.int32, (1, LANES), 1)
    low = lane < HEAD_DIM

    k_all = jnp.concatenate([kp_ref[...], kc_ref[...]], axis=0)
    v_all = jnp.concatenate([vp_ref[...], vc_ref[...]], axis=0)
    zero = jnp.zeros_like(k_all)
    k_sw = pltpu.roll(k_all.astype(F32), HEAD_DIM, axis=1).astype(BF16)
    v_sw = pltpu.roll(v_all.astype(F32), HEAD_DIM, axis=1).astype(BF16)
    lo_f = jnp.broadcast_to(jnp.where(low, 1.0, 0.0), k_all.shape)
    ones_lo = lo_f.astype(BF16)
    ones_hi = (1.0 - lo_f).astype(BF16)
    for g, (kg, vg) in enumerate([((k_all, k_sw), (v_all, v_sw)), ((k_sw, k_all), (v_sw, v_all))]):
        kvar_ref[2 * g] = jnp.where(low, kg[0], zero)
        kvar_ref[2 * g + 1] = jnp.where(low, zero, kg[1])
        vvar_ref[2 * g] = jnp.concatenate([jnp.where(low, vg[0], zero), ones_lo], axis=1)
        vvar_ref[2 * g + 1] = jnp.concatenate([jnp.where(low, zero, vg[1]), ones_hi], axis=1)

    pen = jnp.where(i == 0, NEG_INF, 0.0).astype(F32)

    for j in range(TQ_ATTN // WINDOW):
        r0 = j * WINDOW
        for pair in range(N_ATTN_HEADS // 2):
            qp = q_ref[r0:r0 + WINDOW, pair * PAIR:(pair + 1) * PAIR]
            g = (2 * pair) // GQA_GROUP
            ms, es = [], []
            for o in range(2):
                h = 2 * pair + o
                s = _dot_nt(qp, kvar_ref[2 * g + o, r0:r0 + 2 * WINDOW, :]) + bias_ref[h]
                if j == 0:
                    s = jnp.concatenate([s[:, :WINDOW] + pen, s[:, WINDOW:]], axis=1)
                sink = sinks_ref[h] * LOG2E
                m = jnp.maximum(jnp.max(s, axis=-1, keepdims=True), sink)
                es.append(jnp.exp2(s - m).astype(BF16))
                ms.append(jnp.exp2(sink - m))
            acc = (_dot(es[0], vvar_ref[2 * g, r0:r0 + 2 * WINDOW, :])
                   + _dot(es[1], vvar_ref[2 * g + 1, r0:r0 + 2 * WINDOW, :]))
            den = acc[:, PAIR:] + jnp.where(low, ms[0], ms[1])
            o_ref[r0:r0 + WINDOW, pair * PAIR:(pair + 1) * PAIR] = (acc[:, :PAIR] / den).astype(BF16)


def _attention(qkv3, sinks):
    b, t, _ = qkv3.shape
    nq = TQ_ATTN // WINDOW
    kcol = ATTN_WIDTH // KV_WIDTH
    prev = lambda bi, i: (bi, jnp.maximum(i * nq - 1, 0))
    return pl.pallas_call(
        _attn_kernel,
        grid=(b, t // TQ_ATTN),
        in_specs=[
            pl.BlockSpec(memory_space=pltpu.SMEM),
            pl.BlockSpec((None, TQ_ATTN, ATTN_WIDTH), lambda bi, i: (bi, i, 0)),
            pl.BlockSpec((None, TQ_ATTN, KV_WIDTH), lambda bi, i: (bi, i, kcol)),
            pl.BlockSpec((None, TQ_ATTN, KV_WIDTH), lambda bi, i: (bi, i, kcol + 1)),
            pl.BlockSpec((None, WINDOW, KV_WIDTH), lambda bi, i: prev(bi, i) + (kcol,)),
            pl.BlockSpec((None, WINDOW, KV_WIDTH), lambda bi, i: prev(bi, i) + (kcol + 1,)),
        ],
        out_specs=pl.BlockSpec((None, TQ_ATTN, ATTN_WIDTH), lambda bi, i: (bi, i, 0)),
        out_shape=jax.ShapeDtypeStruct((b, t, ATTN_WIDTH), BF16),
        scratch_shapes=[
            pltpu.VMEM((N_ATTN_HEADS, WINDOW, 2 * WINDOW), F32),
            pltpu.VMEM((2 * N_KV_HEADS, WINDOW + TQ_ATTN, KV_WIDTH), BF16),
            pltpu.VMEM((2 * N_KV_HEADS, WINDOW + TQ_ATTN, 2 * KV_WIDTH), BF16),
        ],
        compiler_params=pltpu.CompilerParams(
            dimension_semantics=("arbitrary", "arbitrary"), vmem_limit_bytes=VMEM_LIMIT),
        name="attn",
    )(sinks, qkv3, qkv3, qkv3, qkv3, qkv3)


V_W0, V_A0, V_KK, V_KA, V_RK, V_LNW, V_LNB = range(7)
N_VECS = 8
GROUP = 4 * CHUNK
QUAD = MXU_WIDTH
N_QUADS = RWKV_WIDTH // QUAD
N_CHUNKS = TM_RWKV // CHUNK
OP_R, OP_A, OP_B, OP_K, OP_V = range(5)
HOOK_PREP = 0
HOOK_STATE = 3


def _rwkv_kernel(zr_ref, mu_ref, vecs_ref, wa_ref, g2_ref, ones_ref, o_ref,
                 state_ref, prev_ref, phi_ref, delta_ref, rw_ref, yc_ref, sin_ref, sout_ref,
                 bonus_ref, gate_ref, y_ref, ops_ref):
    ti = pl.program_id(1)

    @pl.when(ti == 0)
    def _():
        state_ref[...] = jnp.zeros_like(state_ref)
        prev_ref[...] = jnp.zeros_like(prev_ref)

    C = CHUNK
    G = GROUP
    lane = lax.broadcasted_iota(jnp.int32, (1, LANES), 1)
    low = lane < HEAD_DIM
    rowg = lax.broadcasted_iota(jnp.int32, (G, 1), 0)

    r2 = lax.broadcasted_iota(jnp.int32, (G, G), 0)
    c2 = lax.broadcasted_iota(jnp.int32, (G, G), 1)
    same_chunk = (r2 // C) == (c2 // C)
    rp = lax.broadcasted_iota(jnp.int32, (PAIR, PAIR), 0)
    cp = lax.broadcasted_iota(jnp.int32, (PAIR, PAIR), 1)
    same_head = (rp // HEAD_DIM) == (cp // HEAD_DIM)
    r1 = lax.broadcasted_iota(jnp.int32, (C, QUAD), 0)
    c1 = lax.broadcasted_iota(jnp.int32, (C, QUAD), 1) % C
    rs_strict = r1 > c1
    rs_incl = r1 >= c1
    eye_rs = (r1 == c1).astype(F32)
    head_of_lane = lax.broadcasted_iota(jnp.int32, (1, QUAD), 1) // HEAD_DIM
    mid = C // 2 - 1
    emat = jnp.where(same_chunk, (c2 <= r2).astype(F32) - ((c2 % C) <= mid).astype(F32), 0.0).astype(BF16)

    mu = mu_ref[...]
    vec = lambda k: vecs_ref[k:k + 1, :]
    ones_bd = ones_ref[...]
    decay_scale = -math.exp(-0.5)

    def head_sum(x):
        w = ones_bd.shape[0]
        return jnp.concatenate(
            [_dot_split_lhs(x[:, o:o + w], ones_bd, 2) for o in range(0, RWKV_WIDTH, w)], axis=1)

    def stack4(x):
        zero = jnp.zeros_like(x)
        return jnp.concatenate(
            [jnp.where(head_of_lane == h, x, zero) for h in range(QUAD // HEAD_DIM)], axis=0)

    def prep(gi):
        r0 = gi * G
        zc = zr_ref[r0:r0 + G, :].astype(F32)
        zprev = jnp.where(rowg == 0, prev_ref[0:1, :], pltpu.roll(zc, 1, axis=0))
        prev_ref[0:1, :] = zc[G - 1:G, :]
        zs = zc + (zprev - zc) * mu
        xr = zs[:, 0:RWKV_WIDTH]
        xk = zs[:, RWKV_WIDTH:2 * RWKV_WIDTH]
        xv = zs[:, 2 * RWKV_WIDTH:3 * RWKV_WIDTH]
        xwa = zs[:, 3 * RWKV_WIDTH:3 * RWKV_WIDTH + LANES]
        xg = zs[:, 3 * RWKV_WIDTH + LANES:]

        lora = _dot(jnp.where(low, jnp.tanh(xwa), xwa).astype(BF16), wa_ref[...])
        lw = decay_scale * _sigmoid(vec(V_W0) + lora[:, :RWKV_WIDTH])
        a = _sigmoid(vec(V_A0) + lora[:, RWKV_WIDTH:])
        gate_ref[r0:r0 + G, :] = _dot(_sigmoid(xg).astype(BF16), g2_ref[...])

        kk = xk * vec(V_KK)
        kk = kk / jnp.maximum(jnp.sqrt(head_sum(kk * kk)), L2_EPS)
        k = xk * (1.0 + (a - 1.0) * vec(V_KA))
        bonus_ref[r0:r0 + G, :] = head_sum(xr * k * vec(V_RK)) * xv

        d1 = _dot_split_rhs(emat, lw, 3)
        e_neg = jnp.exp(-d1)
        ops_ref[OP_R, r0:r0 + G, :] = (xr * jnp.exp(d1)).astype(BF16)
        ops_ref[OP_A, r0:r0 + G, :] = (-kk * jnp.exp(d1 - lw)).astype(BF16)
        ops_ref[OP_B, r0:r0 + G, :] = (kk * a * e_neg).astype(BF16)
        ops_ref[OP_K, r0:r0 + G, :] = (k * e_neg).astype(BF16)
        ops_ref[OP_V, r0:r0 + G, :] = xv.astype(BF16)
        for j in range(G // C):
            ci = gi * (G // C) + j
            rho = lw[j * C:j * C + 1, :] - d1[j * C:j * C + 1, :]
            sin_ref[ci] = jnp.exp(rho)
            sout_ref[ci] = jnp.exp(d1[(j + 1) * C - 1:(j + 1) * C, :])

    def chains(gi, hooks):
        chains = []
        for j in range(G // C):
            rows = slice(gi * G + j * C, gi * G + (j + 1) * C)
            for q in range(N_QUADS):
                sl = slice(q * QUAD, (q + 1) * QUAD)
                chains.append(dict(ci=gi * (G // C) + j, rows=rows, q=q, sl=sl,
                                   a=ops_ref[OP_A, rows, sl], r=ops_ref[OP_R, rows, sl], b=ops_ref[OP_B, rows, sl],
                                   k=ops_ref[OP_K, rows, sl], v=ops_ref[OP_V, rows, sl]))
        step = 0

        def run_hook():
            nonlocal step
            if step in hooks:
                hooks[step]()
            step += 1

        for ch in chains:
            ch["v_s"] = stack4(ch["v"])
            lg = jnp.concatenate([ch["a"], ch["r"]], axis=0)
            mf = _dot_nt(lg, jnp.concatenate([stack4(ch["b"]), stack4(ch["k"])], axis=0))
            ch["n"] = jnp.where(rs_strict, mf[:C, :QUAD], 0.0)
            ch["a_ak"] = jnp.where(rs_strict, mf[:C, QUAD:], 0.0).astype(BF16)
            ch["a_rb"] = jnp.where(rs_incl, mf[C:, :QUAD], 0.0).astype(BF16)
            ch["a_rk"] = jnp.where(rs_incl, mf[C:, QUAD:], 0.0).astype(BF16)
            ch["t"] = eye_rs + ch["n"]
        run_hook()
        for ch in chains:
            n_b = ch["n"].astype(BF16)
            ch["p"] = _dot(n_b, stack4(n_b)).astype(BF16)
        run_hook()
        pw = 2
        while pw < C:
            last = 2 * pw >= C
            for ch in chains:
                t_b = ch["t"].astype(BF16)
                lhs = t_b if last else jnp.concatenate([ch["p"], t_b], axis=0)
                res = _dot(lhs, stack4(ch["p"]))
                if last:
                    ch["t"] = ch["t"] + res
                else:
                    ch["p"] = res[:C].astype(BF16)
                    ch["t"] = ch["t"] + res[C:]
            run_hook()
            pw *= 2
        for ch in chains:
            av = _dot(ch["a_ak"], ch["v_s"]).astype(BF16)
            wu = _dot(ch["t"].astype(BF16), jnp.concatenate([stack4(ch["a"]), stack4(av)], axis=1))
            ch["w"] = wu[:, :QUAD].astype(BF16)
            ch["u0"] = wu[:, QUAD:].astype(BF16)
        run_hook()
        for ch in chains:
            rows = ch["rows"]
            ru = _dot(ch["a_rb"], jnp.concatenate([stack4(ch["w"]), stack4(ch["u0"])], axis=1))
            rw_ref[rows, ch["sl"]] = (ch["r"].astype(F32) + ru[:, :QUAD]).astype(BF16)
            yc_ref[rows, ch["sl"]] = ru[:, QUAD:] + _dot(ch["a_rk"], ch["v_s"])
            for h2 in range(QUAD // PAIR):
                ps = slice(h2 * PAIR, (h2 + 1) * PAIR)
                p = ch["q"] * (QUAD // PAIR) + h2
                phi = _dot_tn(ch["w"][:, ps], ch["b"][:, ps])
                dlt = _dot_tn(jnp.concatenate([ch["u0"][:, ps], ch["v"][:, ps]], axis=0),
                              jnp.concatenate([ch["b"][:, ps], ch["k"][:, ps]], axis=0))
                phi_ref[ch["ci"], p] = jnp.where(same_head, phi, 0.0).astype(BF16)
                delta_ref[ch["ci"], p] = jnp.where(same_head, dlt, 0.0)

    def state_pass(gi):
        for c in range(gi * (G // C), (gi + 1) * (G // C)):
            rows = slice(c * C, (c + 1) * C)
            s_in = sin_ref[c]
            s_out = sout_ref[c]
            for p in range(N_PAIRS):
                sl = slice(p * PAIR, (p + 1) * PAIR)
                s_p = state_ref[p] * s_in[:, sl]
                s_b = s_p.astype(BF16)
                y_ref[rows, sl] = _dot_nt(rw_ref[rows, sl], s_b) + yc_ref[rows, sl]
                state_ref[p] = (s_p + _dot(s_b, phi_ref[c, p]) + delta_ref[c, p]) * s_out[:, sl]
        blk = slice(gi * G, (gi + 1) * G)
        y = y_ref[blk, :]
        m1 = head_sum(y) * (1.0 / HEAD_DIM)
        dy = y - m1
        var = head_sum(dy * dy) * (1.0 / HEAD_DIM)
        yn = dy * lax.rsqrt(var + GN_EPS) * vec(V_LNW) + vec(V_LNB)
        o_ref[blk, :] = ((yn + bonus_ref[blk, :]) * gate_ref[blk, :]).astype(BF16)

    n_groups = TM_RWKV // G
    prep(0)
    for gi in range(n_groups):
        hooks = {}
        if gi + 1 < n_groups:
            hooks[HOOK_PREP] = lambda gi=gi: prep(gi + 1)
        if gi > 0:
            hooks[HOOK_STATE] = lambda gi=gi: state_pass(gi - 1)
        chains(gi, hooks)
    state_pass(n_groups - 1)


def _rwkv(zr3, mu, vecs, wa_bd, g2, ones_bd):
    b, t, _ = zr3.shape
    const = lambda shape: pl.BlockSpec(shape, lambda bi, i: (0,) * len(shape))
    return pl.pallas_call(
        _rwkv_kernel,
        grid=(b, t // TM_RWKV),
        in_specs=[
            pl.BlockSpec((None, TM_RWKV, RWKV_SHIFT_WIDTH), lambda bi, i: (bi, i, 0)),
            const((1, RWKV_SHIFT_WIDTH)),
            const((N_VECS, RWKV_WIDTH)),
            const((LANES, 2 * RWKV_WIDTH)),
            const((GATE_LORA, RWKV_WIDTH)),
            const((MXU_WIDTH, MXU_WIDTH)),
        ],
        out_specs=pl.BlockSpec((None, TM_RWKV, RWKV_WIDTH), lambda bi, i: (bi, i, 0)),
        out_shape=jax.ShapeDtypeStruct((b, t, RWKV_WIDTH), BF16),
        scratch_shapes=[
            pltpu.VMEM((N_PAIRS, PAIR, PAIR), F32),
            pltpu.VMEM((8, RWKV_SHIFT_WIDTH), F32),
            pltpu.VMEM((N_CHUNKS, N_PAIRS, PAIR, PAIR), BF16),
            pltpu.VMEM((N_CHUNKS, N_PAIRS, PAIR, PAIR), F32),
            pltpu.VMEM((TM_RWKV, RWKV_WIDTH), BF16),
            pltpu.VMEM((TM_RWKV, RWKV_WIDTH), F32),
            pltpu.VMEM((N_CHUNKS, 1, RWKV_WIDTH), F32),
            pltpu.VMEM((N_CHUNKS, 1, RWKV_WIDTH), F32),
            pltpu.VMEM((TM_RWKV, RWKV_WIDTH), F32),
            pltpu.VMEM((TM_RWKV, RWKV_WIDTH), F32),
            pltpu.VMEM((TM_RWKV, RWKV_WIDTH), F32),
            pltpu.VMEM((5, TM_RWKV, RWKV_WIDTH), BF16),
        ],
        compiler_params=pltpu.CompilerParams(
            dimension_semantics=("arbitrary", "arbitrary"), vmem_limit_bytes=VMEM_LIMIT),
        name="rwkv",
    )(zr3, mu, vecs, wa_bd, g2, ones_bd)


def _mlp_kernel(x_ref, at_ref, rw_ref, wo_ref, g1_ref, wu_ref, wd_ref, g2_ref, o_ref, act_ref):
    x1 = (x_ref[...] + _dot(at_ref[...], wo_ref[:ATTN_WIDTH, :])
          + _dot(rw_ref[...], wo_ref[ATTN_WIDTH:, :]))
    h = _rms_norm(x1, g1_ref[...]).astype(BF16)
    for f in range(D_FF // TF_MLP):
        sl = slice(f * TF_MLP, (f + 1) * TF_MLP)
        up = jnp.maximum(_dot(h, wu_ref[:, sl]), 0.0)
        act_ref[:, sl] = (up * up).astype(BF16)
    x2 = x1 + _dot(act_ref[...], wd_ref[...])
    o_ref[...] = _rms_norm(x2, g2_ref[...])


def _mlp(x2, attn, rwkv, w_out, g_mlp, w_up, w_down, g_final):
    n = x2.shape[0]
    const = lambda shape: pl.BlockSpec(shape, lambda i: (0, 0), pipeline_mode=pl.Buffered(1))
    return pl.pallas_call(
        _mlp_kernel,
        grid=(n // TM_MLP,),
        in_specs=[
            pl.BlockSpec((TM_MLP, D_MODEL), lambda i: (i, 0)),
            pl.BlockSpec((TM_MLP, ATTN_WIDTH), lambda i: (i, 0)),
            pl.BlockSpec((TM_MLP, RWKV_WIDTH), lambda i: (i, 0)),
            const((D_MODEL, D_MODEL)),
            const((1, D_MODEL)),
            const((D_MODEL, D_FF)),
            const((D_FF, D_MODEL)),
            const((1, D_MODEL)),
        ],
        out_specs=pl.BlockSpec((TM_MLP, D_MODEL), lambda i: (i, 0)),
        out_shape=jax.ShapeDtypeStruct((n, D_MODEL), F32),
        scratch_shapes=[pltpu.VMEM((TM_MLP, D_FF), BF16)],
        compiler_params=pltpu.CompilerParams(
            dimension_semantics=("arbitrary",), vmem_limit_bytes=VMEM_LIMIT),
        name="mlp",
    )(x2, attn, rwkv, w_out, g_mlp, w_up, w_down, g_final)


def _layer(x, attn_norm_g, w_in, attn_sinks, rwkv_mu, w0, w2, a0, a2, g2, k_k, k_a, r_k,
           ln_x_w, ln_x_b, w_out, mlp_norm_g, w_up, w_down):
    b, t, d = x.shape
    x2 = x.reshape(b * t, d)
    row = lambda v: v.reshape(1, -1)

    qkv, zr = _inproj(x2, row(attn_norm_g), w_in.astype(BF16))
    attn = _attention(qkv.reshape(b, t, QKV_WIDTH), attn_sinks)

    zero = jnp.zeros((DECAY_LORA, RWKV_WIDTH), F32)
    wa_bd = jnp.concatenate(
        [jnp.concatenate([w2, zero], axis=1), jnp.concatenate([zero, a2], axis=1)], axis=0).astype(BF16)
    vecs = jnp.stack([w0, a0, k_k, k_a, r_k, ln_x_w, ln_x_b, jnp.zeros_like(w0)], axis=0)
    head_id = jnp.arange(MXU_WIDTH) // HEAD_DIM
    ones_bd = (head_id[:, None] == head_id[None, :]).astype(BF16)
    rwkv = _rwkv(zr.reshape(b, t, RWKV_SHIFT_WIDTH), row(rwkv_mu), vecs, wa_bd, g2.astype(BF16), ones_bd)

    return x2, attn.reshape(b * t, ATTN_WIDTH), rwkv.reshape(b * t, RWKV_WIDTH)


def kernel(x, attn_norm_g, w_in, attn_sinks, rwkv_mu, w0, w2, a0, a2, g2, k_k, k_a, r_k, ln_x_w, ln_x_b,
           w_out, mlp_norm_g, w_up, w_down, final_norm_g):
    assert attn_norm_g.shape[0] == 1, "one trunk layer"
    b, t, d = x.shape
    x2, attn, rwkv = _layer(x, attn_norm_g[0], w_in[0], attn_sinks[0], rwkv_mu[0], w0[0], w2[0], a0[0], a2[0],
                            g2[0], k_k[0], k_a[0], r_k[0], ln_x_w[0], ln_x_b[0], w_out[0], mlp_norm_g[0],
                            w_up[0], w_down[0])
    out = _mlp(x2, attn, rwkv, w_out[0].astype(BF16), mlp_norm_g[0].reshape(1, -1), w_up[0].astype(BF16),
               w_down[0].astype(BF16), final_norm_g.reshape(1, -1))
    return out.reshape(b, t, d)
```

```python
import math

import jax
import jax.numpy as jnp
from jax import lax
from jax.experimental import pallas as pl
from jax.experimental.pallas import tpu as pltpu

F32 = jnp.float32
BF16 = jnp.bfloat16

D_MODEL = 1024
HEAD_DIM = 64
N_ATTN_HEADS = 8
N_KV_HEADS = 2
GQA_GROUP = N_ATTN_HEADS // N_KV_HEADS
ATTN_WIDTH = N_ATTN_HEADS * HEAD_DIM
KV_WIDTH = N_KV_HEADS * HEAD_DIM
QKV_WIDTH = ATTN_WIDTH + 2 * KV_WIDTH
N_RWKV_HEADS = 8
RWKV_WIDTH = N_RWKV_HEADS * HEAD_DIM
WINDOW = 128
DECAY_LORA = 64
ICLR_LORA = 64
GATE_LORA = 128
RWKV_SHIFT_WIDTH = 3 * RWKV_WIDTH + DECAY_LORA + ICLR_LORA + GATE_LORA
IN_WIDTH = QKV_WIDTH + RWKV_SHIFT_WIDTH
D_FF = 4 * D_MODEL
RMS_EPS = 1e-6
GN_EPS = 64e-5
L2_EPS = 1e-12
NEG_INF = -1e30
LOG2E = math.log2(math.e)
Q_SCALE = HEAD_DIM ** -0.5 * LOG2E

LANES = 128
MXU_WIDTH = 256
PAIR = 2 * HEAD_DIM
N_PAIRS = RWKV_WIDTH // PAIR
CHUNK = 64
VMEM_LIMIT = 56 * 1024 * 1024

TM_PROJ = 512
TQ_ATTN = 512
TM_RWKV = 512
TM_MLP = 512
TF_MLP = 512

NT_DIMS = (((1,), (1,)), ((), ()))
TN_DIMS = (((0,), (0,)), ((), ()))


def _dot(a, b):
    return jnp.dot(a, b, preferred_element_type=F32)


def _dot_nt(a, b):
    return lax.dot_general(a, b, NT_DIMS, preferred_element_type=F32)


def _dot_tn(a, b):
    return lax.dot_general(a, b, TN_DIMS, preferred_element_type=F32)


def _split_bf16(x, terms):
    parts = []
    rem = x
    for _ in range(terms - 1):
        p = rem.astype(BF16)
        parts.append(p)
        rem = rem - p.astype(F32)
    parts.append(rem.astype(BF16))
    return parts


def _dot_split_lhs(x, rhs_bf16, terms):
    acc = None
    for p in _split_bf16(x, terms):
        d = _dot(p, rhs_bf16)
        acc = d if acc is None else acc + d
    return acc


def _dot_split_rhs(lhs_bf16, x, terms):
    acc = None
    for p in _split_bf16(x, terms):
        d = _dot(lhs_bf16, p)
        acc = d if acc is None else acc + d
    return acc


def _sigmoid(x):
    return 1.0 / (1.0 + jnp.exp(-x))


def _rms_norm(x, g):
    ms = jnp.mean(x * x, axis=-1, keepdims=True)
    return x * lax.rsqrt(ms + RMS_EPS) * g


def _inproj_kernel(x_ref, g_ref, w_ref, qkv_ref, zr_ref):
    h = _rms_norm(x_ref[...], g_ref[...]).astype(BF16)
    qkv_ref[:, :ATTN_WIDTH] = (_dot(h, w_ref[:, :ATTN_WIDTH]) * Q_SCALE).astype(BF16)
    qkv_ref[:, ATTN_WIDTH:] = _dot(h, w_ref[:, ATTN_WIDTH:QKV_WIDTH]).astype(BF16)
    zr_ref[...] = _dot(h, w_ref[:, QKV_WIDTH:]).astype(BF16)


def _inproj(x2, g, w_in):
    n = x2.shape[0]
    return pl.pallas_call(
        _inproj_kernel,
        grid=(n // TM_PROJ,),
        in_specs=[
            pl.BlockSpec((TM_PROJ, D_MODEL), lambda i: (i, 0)),
            pl.BlockSpec((1, D_MODEL), lambda i: (0, 0)),
            pl.BlockSpec((D_MODEL, IN_WIDTH), lambda i: (0, 0)),
        ],
        out_specs=[
            pl.BlockSpec((TM_PROJ, QKV_WIDTH), lambda i: (i, 0)),
            pl.BlockSpec((TM_PROJ, RWKV_SHIFT_WIDTH), lambda i: (i, 0)),
        ],
        out_shape=[
            jax.ShapeDtypeStruct((n, QKV_WIDTH), BF16),
            jax.ShapeDtypeStruct((n, RWKV_SHIFT_WIDTH), BF16),
        ],
        compiler_params=pltpu.CompilerParams(
            dimension_semantics=("arbitrary",), vmem_limit_bytes=VMEM_LIMIT),
        name="inproj",
    )(x2, g, w_in)


def _attn_kernel(sinks_ref, q_ref, kc_ref, vc_ref, kp_ref, vp_ref, o_ref, bias_ref, kvar_ref, vvar_ref):
    i = pl.program_id(1)

    @pl.when((pl.program_id(0) == 0) & (i == 0))
    def _():
        qi = lax.broadcasted_iota(jnp.int32, (WINDOW, 2 * WINDOW), 0)
        kj = lax.broadcasted_iota(jnp.int32, (WINDOW, 2 * WINDOW), 1)
        dist = qi - kj + WINDOW
        valid = (dist >= 0) & (dist < WINDOW)
        distf = dist.astype(F32)
        for h in range(N_ATTN_HEADS):
            slope = 2.0 ** (-8.0 * (h + 1) / N_ATTN_HEADS)
            bias_ref[h] = jnp.where(valid, (-slope * LOG2E) * distf, NEG_INF)

    lane = lax.broadcasted_iota(jnp.int32, (1, LANES), 1)
    low = lane < HEAD_DIM

    k_all = jnp.concatenate([kp_ref[...], kc_ref[...]], axis=0)
    v_all = jnp.concatenate([vp_ref[...], vc_ref[...]], axis=0)
    zero = jnp.zeros_like(k_all)
    k_sw = pltpu.roll(k_all.astype(F32), HEAD_DIM, axis=1).astype(BF16)
    v_sw = pltpu.roll(v_all.astype(F32), HEAD_DIM, axis=1).astype(BF16)
    lo_f = jnp.broadcast_to(jnp.where(low, 1.0, 0.0), k_all.shape)
    ones_lo = lo_f.astype(BF16)
    ones_hi = (1.0 - lo_f).astype(BF16)
    for g, (kg, vg) in enumerate([((k_all, k_sw), (v_all, v_sw)), ((k_sw, k_all), (v_sw, v_all))]):
        kvar_ref[2 * g] = jnp.where(low, kg[0], zero)
        kvar_ref[2 * g + 1] = jnp.where(low, zero, kg[1])
        vvar_ref[2 * g] = jnp.concatenate([jnp.where(low, vg[0], zero), ones_lo], axis=1)
        vvar_ref[2 * g + 1] = jnp.concatenate([jnp.where(low, zero, vg[1]), ones_hi], axis=1)

    pen = jnp.where(i == 0, NEG_INF, 0.0).astype(F32)

    for j in range(TQ_ATTN // WINDOW):
        r0 = j * WINDOW
        for pair in range(N_ATTN_HEADS // 2):
            qp = q_ref[r0:r0 + WINDOW, pair * PAIR:(pair + 1) * PAIR]
            g = (2 * pair) // GQA_GROUP
            ms, es = [], []
            for o in range(2):
                h = 2 * pair + o
                s = _dot_nt(qp, kvar_ref[2 * g + o, r0:r0 + 2 * WINDOW, :]) + bias_ref[h]
                if j == 0:
                    s = jnp.concatenate([s[:, :WINDOW] + pen, s[:, WINDOW:]], axis=1)
                sink = sinks_ref[h] * LOG2E
                m = jnp.maximum(jnp.max(s, axis=-1, keepdims=True), sink)
                es.append(jnp.exp2(s - m).astype(BF16))
                ms.append(jnp.exp2(sink - m))
            acc = (_dot(es[0], vvar_ref[2 * g, r0:r0 + 2 * WINDOW, :])
                   + _dot(es[1], vvar_ref[2 * g + 1, r0:r0 + 2 * WINDOW, :]))
            den = acc[:, PAIR:] + jnp.where(low, ms[0], ms[1])
            o_ref[r0:r0 + WINDOW, pair * PAIR:(pair + 1) * PAIR] = (acc[:, :PAIR] / den).astype(BF16)


def _attention(qkv3, sinks):
    b, t, _ = qkv3.shape
    nq = TQ_ATTN // WINDOW
    kcol = ATTN_WIDTH // KV_WIDTH
    prev = lambda bi, i: (bi, jnp.maximum(i * nq - 1, 0))
    return pl.pallas_call(
        _attn_kernel,
        grid=(b, t // TQ_ATTN),
        in_specs=[
            pl.BlockSpec(memory_space=pltpu.SMEM),
            pl.BlockSpec((None, TQ_ATTN, ATTN_WIDTH), lambda bi, i: (bi, i, 0)),
            pl.BlockSpec((None, TQ_ATTN, KV_WIDTH), lambda bi, i: (bi, i, kcol)),
            pl.BlockSpec((None, TQ_ATTN, KV_WIDTH), lambda bi, i: (bi, i, kcol + 1)),
            pl.BlockSpec((None, WINDOW, KV_WIDTH), lambda bi, i: prev(bi, i) + (kcol,)),
            pl.BlockSpec((None, WINDOW, KV_WIDTH), lambda bi, i: prev(bi, i) + (kcol + 1,)),
        ],
        out_specs=pl.BlockSpec((None, TQ_ATTN, ATTN_WIDTH), lambda bi, i: (bi, i, 0)),
        out_shape=jax.ShapeDtypeStruct((b, t, ATTN_WIDTH), BF16),
        scratch_shapes=[
            pltpu.VMEM((N_ATTN_HEADS, WINDOW, 2 * WINDOW), F32),
            pltpu.VMEM((2 * N_KV_HEADS, WINDOW + TQ_ATTN, KV_WIDTH), BF16),
            pltpu.VMEM((2 * N_KV_HEADS, WINDOW + TQ_ATTN, 2 * KV_WIDTH), BF16),
        ],
        compiler_params=pltpu.CompilerParams(
            dimension_semantics=("arbitrary", "arbitrary"), vmem_limit_bytes=VMEM_LIMIT),
        name="attn",
    )(sinks, qkv3, qkv3, qkv3, qkv3, qkv3)


V_W0, V_A0, V_KK, V_KA, V_RK, V_LNW, V_LNB = range(7)
N_VECS = 8
GROUP = 4 * CHUNK
QUAD = MXU_WIDTH
N_QUADS = RWKV_WIDTH // QUAD
N_CHUNKS = TM_RWKV // CHUNK
OP_R, OP_A, OP_B, OP_K, OP_V = range(5)
N_HOOKS = 64
PREP_STAGES = 6
PREP_ROWS = 2 * CHUNK


def _rwkv_kernel(zr_ref, mu_ref, vecs_ref, wa_ref, g2_ref, ones_ref, o_ref,
                 state_ref, prev_ref, phi_ref, delta_ref, rw_ref, yc_ref, sin_ref, sout_ref,
                 bonus_ref, gate_ref, y_ref, ops_ref, dcol_ref):
    ti = pl.program_id(1)

    @pl.when(ti == 0)
    def _():
        state_ref[...] = jnp.zeros_like(state_ref)
        prev_ref[...] = jnp.zeros_like(prev_ref)

    C = CHUNK
    G = GROUP
    lane = lax.broadcasted_iota(jnp.int32, (1, LANES), 1)
    low = lane < HEAD_DIM
    P = PREP_ROWS
    rowp = lax.broadcasted_iota(jnp.int32, (P, 1), 0)

    r2 = lax.broadcasted_iota(jnp.int32, (P, P), 0)
    c2 = lax.broadcasted_iota(jnp.int32, (P, P), 1)
    same_chunk = (r2 // C) == (c2 // C)
    rp = lax.broadcasted_iota(jnp.int32, (PAIR, PAIR), 0)
    cp = lax.broadcasted_iota(jnp.int32, (PAIR, PAIR), 1)
    same_head = (rp // HEAD_DIM) == (cp // HEAD_DIM)
    eye_p = rp == cp
    r1 = lax.broadcasted_iota(jnp.int32, (C, QUAD), 0)
    c1 = lax.broadcasted_iota(jnp.int32, (C, QUAD), 1) % C
    rs_strict = r1 > c1
    rs_incl = r1 >= c1
    eye_rs = (r1 == c1).astype(F32)
    head_of_lane = lax.broadcasted_iota(jnp.int32, (1, QUAD), 1) // HEAD_DIM
    mid = C // 2 - 1
    emat = jnp.where(same_chunk, (c2 <= r2).astype(F32) - ((c2 % C) <= mid).astype(F32), 0.0).astype(BF16)

    mu = mu_ref[...]
    vec = lambda k: vecs_ref[k:k + 1, :]
    ones_bd = ones_ref[...]
    decay_scale = -math.exp(-0.5)

    def head_sum(x):
        w = ones_bd.shape[0]
        n = x.shape[0]
        hi_lo = jnp.concatenate(_split_bf16(x, 2), axis=0)
        sums = [_dot(hi_lo[:, o:o + w], ones_bd) for o in range(0, RWKV_WIDTH, w)]
        return jnp.concatenate([s[:n] + s[n:] for s in sums], axis=1)

    def stack4(x):
        zero = jnp.zeros_like(x)
        return jnp.concatenate(
            [jnp.where(head_of_lane == h, x, zero) for h in range(QUAD // HEAD_DIM)], axis=0)

    def prep(pi):
        r0 = pi * P
        zc = zr_ref[r0:r0 + P, :].astype(F32)
        zprev = jnp.where(rowp == 0, prev_ref[0:1, :], pltpu.roll(zc, 1, axis=0))
        prev_ref[0:1, :] = zc[P - 1:P, :]
        zs = zc + (zprev - zc) * mu
        xr = zs[:, 0:RWKV_WIDTH]
        xk = zs[:, RWKV_WIDTH:2 * RWKV_WIDTH]
        xv = zs[:, 2 * RWKV_WIDTH:3 * RWKV_WIDTH]
        xwa = zs[:, 3 * RWKV_WIDTH:3 * RWKV_WIDTH + LANES]
        xg = zs[:, 3 * RWKV_WIDTH + LANES:]
        yield

        lora = _dot(jnp.where(low, jnp.tanh(xwa), xwa).astype(BF16), wa_ref[...])
        gate_ref[r0:r0 + P, :] = _dot(_sigmoid(xg).astype(BF16), g2_ref[...])
        kk = xk * vec(V_KK)
        kk_norm2 = head_sum(kk * kk)
        yield

        lw = decay_scale * _sigmoid(vec(V_W0) + lora[:, :RWKV_WIDTH])
        a = _sigmoid(vec(V_A0) + lora[:, RWKV_WIDTH:])
        kk = kk / jnp.maximum(jnp.sqrt(kk_norm2), L2_EPS)
        k = xk * (1.0 + (a - 1.0) * vec(V_KA))
        yield

        bonus_ref[r0:r0 + P, :] = head_sum(xr * k * vec(V_RK)) * xv
        d1 = _dot_split_rhs(emat, lw, 2)
        yield

        e_neg = jnp.exp(-d1)
        ops_ref[OP_R, r0:r0 + P, :] = (xr * jnp.exp(d1)).astype(BF16)
        ops_ref[OP_A, r0:r0 + P, :] = (-kk * jnp.exp(d1 - lw)).astype(BF16)
        yield

        ops_ref[OP_B, r0:r0 + P, :] = (kk * a * e_neg).astype(BF16)
        ops_ref[OP_K, r0:r0 + P, :] = (k * e_neg).astype(BF16)
        ops_ref[OP_V, r0:r0 + P, :] = xv.astype(BF16)
        for j in range(P // C):
            ci = pi * (P // C) + j
            rho = lw[j * C:j * C + 1, :] - d1[j * C:j * C + 1, :]
            sin_ref[ci] = jnp.exp(rho)
            sout_ref[ci] = jnp.exp(d1[(j + 1) * C - 1:(j + 1) * C, :])

    def prep_work(pi):
        gen = prep(pi)
        return [lambda: next(gen, None)] * PREP_STAGES

    def chains(gi, hooks):
        chains = []
        for j in range(G // C):
            rows = slice(gi * G + j * C, gi * G + (j + 1) * C)
            for q in range(N_QUADS):
                sl = slice(q * QUAD, (q + 1) * QUAD)
                chains.append(dict(ci=gi * (G // C) + j, rows=rows, q=q, sl=sl,
                                   a=ops_ref[OP_A, rows, sl], r=ops_ref[OP_R, rows, sl], b=ops_ref[OP_B, rows, sl],
                                   k=ops_ref[OP_K, rows, sl], v=ops_ref[OP_V, rows, sl]))
        step = 0

        def run_hook():
            nonlocal step
            for item in hooks.get(step, []):
                item()
            step += 1

        for ch in chains:
            lg = jnp.concatenate([ch["a"], ch["r"]], axis=0)
            mf = _dot_nt(lg, jnp.concatenate([stack4(ch["b"]), stack4(ch["k"])], axis=0))
            ch["n"] = jnp.where(rs_strict, mf[:C, :QUAD], 0.0)
            ch["a_ak"] = jnp.where(rs_strict, mf[:C, QUAD:], 0.0).astype(BF16)
            ch["a_rb"] = jnp.where(rs_incl, mf[C:, :QUAD], 0.0).astype(BF16)
            ch["a_rk"] = jnp.where(rs_incl, mf[C:, QUAD:], 0.0).astype(BF16)
            ch["t"] = eye_rs + ch["n"]
            run_hook()
        for ch in chains:
            n_b = ch["n"].astype(BF16)
            ch["p"] = _dot(n_b, stack4(n_b)).astype(BF16)
            run_hook()
        pw = 2
        while pw < C:
            last = 2 * pw >= C
            for ch in chains:
                t_b = ch["t"].astype(BF16)
                lhs = t_b if last else jnp.concatenate([ch["p"], t_b], axis=0)
                res = _dot(lhs, stack4(ch["p"]))
                if last:
                    ch["t"] = ch["t"] + res
                else:
                    ch["p"] = res[:C].astype(BF16)
                    ch["t"] = ch["t"] + res[C:]
                run_hook()
            pw *= 2
        for ch in chains:
            akv = _dot(jnp.concatenate([ch["a_ak"], ch["a_rk"]], axis=0), stack4(ch["v"]))
            t_b = ch["t"].astype(BF16)
            art = _dot(ch["a_rb"], stack4(t_b)).astype(BF16)
            wu = _dot(jnp.concatenate([t_b, art], axis=0),
                      jnp.concatenate([stack4(ch["a"]), stack4(akv[:C].astype(BF16))], axis=1))
            ch["w"] = wu[:C, :QUAD]
            ch["u0"] = wu[:C, QUAD:].astype(BF16)
            ch["rw"] = ch["r"].astype(F32) + wu[C:, :QUAD]
            yc_ref[ch["rows"], ch["sl"]] = wu[C:, QUAD:] + akv[C:]
            run_hook()
        for ch in chains:
            s_in = sin_ref[ch["ci"]][:, ch["sl"]]
            s_out = sout_ref[ch["ci"]][:, ch["sl"]]
            rw_ref[ch["rows"], ch["sl"]] = (ch["rw"] * s_in).astype(BF16)
            w_in = (ch["w"] * s_in).astype(BF16)
            b_out = (ch["b"].astype(F32) * s_out).astype(BF16)
            k_out = (ch["k"].astype(F32) * s_out).astype(BF16)
            d_row = jnp.broadcast_to(s_in * s_out, (PAIR, QUAD))
            for h2 in range(QUAD // PAIR):
                ps = slice(h2 * PAIR, (h2 + 1) * PAIR)
                p = ch["q"] * (QUAD // PAIR) + h2
                phi_t = _dot_tn(b_out[:, ps], w_in[:, ps])
                dlt_t = _dot_tn(jnp.concatenate([b_out[:, ps], k_out[:, ps]], axis=0),
                                jnp.concatenate([ch["u0"][:, ps], ch["v"][:, ps]], axis=0))
                phi_ref[ch["ci"], p] = jnp.where(same_head, phi_t, 0.0).astype(BF16)
                delta_ref[ch["ci"], p] = jnp.where(same_head, dlt_t, 0.0)
                dcol_ref[ch["ci"], p] = jnp.sum(jnp.where(eye_p, d_row[:, ps], 0.0), axis=1, keepdims=True)

    def state_step(c):
        rows = slice(c * C, (c + 1) * C)
        for p in range(N_PAIRS):
            sl = slice(p * PAIR, (p + 1) * PAIR)
            h = state_ref[p]
            res = _dot(jnp.concatenate([rw_ref[rows, sl], phi_ref[c, p]], axis=0), h.astype(BF16))
            y_ref[rows, sl] = res[:C] + yc_ref[rows, sl]
            state_ref[p] = dcol_ref[c, p] * h + res[C:] + delta_ref[c, p]

    def finish(gi):
        blk = slice(gi * G, (gi + 1) * G)
        y = y_ref[blk, :]
        m1 = head_sum(y) * (1.0 / HEAD_DIM)
        dy = y - m1
        var = head_sum(dy * dy) * (1.0 / HEAD_DIM)
        yn = dy * lax.rsqrt(var + GN_EPS) * vec(V_LNW) + vec(V_LNB)
        o_ref[blk, :] = ((yn + bonus_ref[blk, :]) * gate_ref[blk, :]).astype(BF16)

    def tail_work(gi):
        return [lambda c=c: state_step(c) for c in range(gi * (G // C), (gi + 1) * (G // C))] + [
            lambda: finish(gi)]

    n_groups = TM_RWKV // G
    per_group = G // P
    for pi in range(per_group):
        for item in prep_work(pi):
            item()
    for gi in range(n_groups):
        work = []
        if gi + 1 < n_groups:
            for pi in range((gi + 1) * per_group, (gi + 2) * per_group):
                work += prep_work(pi)
        if gi > 0:
            work += tail_work(gi - 1)
        hooks = {}
        for i, item in enumerate(work):
            hooks.setdefault(i * N_HOOKS // len(work), []).append(item)
        chains(gi, hooks)
    for item in tail_work(n_groups - 1):
        item()


def _rwkv(zr3, mu, vecs, wa_bd, g2, ones_bd):
    b, t, _ = zr3.shape
    const = lambda shape: pl.BlockSpec(shape, lambda bi, i: (0,) * len(shape))
    return pl.pallas_call(
        _rwkv_kernel,
        grid=(b, t // TM_RWKV),
        in_specs=[
            pl.BlockSpec((None, TM_RWKV, RWKV_SHIFT_WIDTH), lambda bi, i: (bi, i, 0)),
            const((1, RWKV_SHIFT_WIDTH)),
            const((N_VECS, RWKV_WIDTH)),
            const((LANES, 2 * RWKV_WIDTH)),
            const((GATE_LORA, RWKV_WIDTH)),
            const((MXU_WIDTH, MXU_WIDTH)),
        ],
        out_specs=pl.BlockSpec((None, TM_RWKV, RWKV_WIDTH), lambda bi, i: (bi, i, 0)),
        out_shape=jax.ShapeDtypeStruct((b, t, RWKV_WIDTH), BF16),
        scratch_shapes=[
            pltpu.VMEM((N_PAIRS, PAIR, PAIR), F32),
            pltpu.VMEM((8, RWKV_SHIFT_WIDTH), F32),
            pltpu.VMEM((N_CHUNKS, N_PAIRS, PAIR, PAIR), BF16),
            pltpu.VMEM((N_CHUNKS, N_PAIRS, PAIR, PAIR), F32),
            pltpu.VMEM((TM_RWKV, RWKV_WIDTH), BF16),
            pltpu.VMEM((TM_RWKV, RWKV_WIDTH), F32),
            pltpu.VMEM((N_CHUNKS, 1, RWKV_WIDTH), F32),
            pltpu.VMEM((N_CHUNKS, 1, RWKV_WIDTH), F32),
            pltpu.VMEM((TM_RWKV, RWKV_WIDTH), F32),
            pltpu.VMEM((TM_RWKV, RWKV_WIDTH), F32),
            pltpu.VMEM((TM_RWKV, RWKV_WIDTH), F32),
            pltpu.VMEM((5, TM_RWKV, RWKV_WIDTH), BF16),
            pltpu.VMEM((N_CHUNKS, N_PAIRS, PAIR, 1), F32),
        ],
        compiler_params=pltpu.CompilerParams(
            dimension_semantics=("arbitrary", "arbitrary"), vmem_limit_bytes=VMEM_LIMIT),
        name="rwkv",
    )(zr3, mu, vecs, wa_bd, g2, ones_bd)


def _mlp_kernel(x_ref, at_ref, rw_ref, wo_ref, g1_ref, wu_ref, wd_ref, g2_ref, o_ref, act_ref):
    x1 = (x_ref[...] + _dot(at_ref[...], wo_ref[:ATTN_WIDTH, :])
          + _dot(rw_ref[...], wo_ref[ATTN_WIDTH:, :]))
    h = _rms_norm(x1, g1_ref[...]).astype(BF16)
    for f in range(D_FF // TF_MLP):
        sl = slice(f * TF_MLP, (f + 1) * TF_MLP)
        up = jnp.maximum(_dot(h, wu_ref[:, sl]), 0.0)
        act_ref[:, sl] = (up * up).astype(BF16)
    x2 = x1 + _dot(act_ref[...], wd_ref[...])
    o_ref[...] = _rms_norm(x2, g2_ref[...])


def _mlp(x2, attn, rwkv, w_out, g_mlp, w_up, w_down, g_final):
    n = x2.shape[0]
    const = lambda shape: pl.BlockSpec(shape, lambda i: (0, 0), pipeline_mode=pl.Buffered(1))
    return pl.pallas_call(
        _mlp_kernel,
        grid=(n // TM_MLP,),
        in_specs=[
            pl.BlockSpec((TM_MLP, D_MODEL), lambda i: (i, 0)),
            pl.BlockSpec((TM_MLP, ATTN_WIDTH), lambda i: (i, 0)),
            pl.BlockSpec((TM_MLP, RWKV_WIDTH), lambda i: (i, 0)),
            const((D_MODEL, D_MODEL)),
            const((1, D_MODEL)),
            const((D_MODEL, D_FF)),
            const((D_FF, D_MODEL)),
            const((1, D_MODEL)),
        ],
        out_specs=pl.BlockSpec((TM_MLP, D_MODEL), lambda i: (i, 0)),
        out_shape=jax.ShapeDtypeStruct((n, D_MODEL), F32),
        scratch_shapes=[pltpu.VMEM((TM_MLP, D_FF), BF16)],
        compiler_params=pltpu.CompilerParams(
            dimension_semantics=("arbitrary",), vmem_limit_bytes=VMEM_LIMIT),
        name="mlp",
    )(x2, attn, rwkv, w_out, g_mlp, w_up, w_down, g_final)


def _layer(x, attn_norm_g, w_in, attn_sinks, rwkv_mu, w0, w2, a0, a2, g2, k_k, k_a, r_k,
           ln_x_w, ln_x_b, w_out, mlp_norm_g, w_up, w_down):
    b, t, d = x.shape
    x2 = x.reshape(b * t, d)
    row = lambda v: v.reshape(1, -1)

    qkv, zr = _inproj(x2, row(attn_norm_g), w_in.astype(BF16))
    attn = _attention(qkv.reshape(b, t, QKV_WIDTH), attn_sinks)

    zero = jnp.zeros((DECAY_LORA, RWKV_WIDTH), F32)
    wa_bd = jnp.concatenate(
        [jnp.concatenate([w2, zero], axis=1), jnp.concatenate([zero, a2], axis=1)], axis=0).astype(BF16)
    vecs = jnp.stack([w0, a0, k_k, k_a, r_k, ln_x_w, ln_x_b, jnp.zeros_like(w0)], axis=0)
    head_id = jnp.arange(MXU_WIDTH) // HEAD_DIM
    ones_bd = (head_id[:, None] == head_id[None, :]).astype(BF16)
    rwkv = _rwkv(zr.reshape(b, t, RWKV_SHIFT_WIDTH), row(rwkv_mu), vecs, wa_bd, g2.astype(BF16), ones_bd)

    return x2, attn.reshape(b * t, ATTN_WIDTH), rwkv.reshape(b * t, RWKV_WIDTH)


def kernel(x, attn_norm_g, w_in, attn_sinks, rwkv_mu, w0, w2, a0, a2, g2, k_k, k_a, r_k, ln_x_w, ln_x_b,
           w_out, mlp_norm_g, w_up, w_down, final_norm_g):
    assert attn_norm_g.shape[0] == 1, "one trunk layer"
    b, t, d = x.shape
    x2, attn, rwkv = _layer(x, attn_norm_g[0], w_in[0], attn_sinks[0], rwkv_mu[0], w0[0], w2[0], a0[0], a2[0],
                            g2[0], k_k[0], k_a[0], r_k[0], ln_x_w[0], ln_x_b[0], w_out[0], mlp_norm_g[0],
                            w_up[0], w_down[0])
    out = _mlp(x2, attn, rwkv, w_out[0].astype(BF16), mlp_norm_g[0].reshape(1, -1), w_up[0].astype(BF16),
               w_down[0].astype(BF16), final_norm_g.reshape(1, -1))
    return out.reshape(b, t, d)
```

```python
import functools
import math

import jax
import jax.numpy as jnp
from jax import lax
from jax.experimental import pallas as pl
from jax.experimental.pallas import tpu as pltpu

F32 = jnp.float32
BF16 = jnp.bfloat16

D_MODEL = 1024
HEAD_DIM = 64
N_ATTN_HEADS = 8
N_KV_HEADS = 2
GQA_GROUP = N_ATTN_HEADS // N_KV_HEADS
ATTN_WIDTH = N_ATTN_HEADS * HEAD_DIM
KV_WIDTH = N_KV_HEADS * HEAD_DIM
QKV_WIDTH = ATTN_WIDTH + 2 * KV_WIDTH
N_RWKV_HEADS = 8
RWKV_WIDTH = N_RWKV_HEADS * HEAD_DIM
WINDOW = 128
DECAY_LORA = 64
ICLR_LORA = 64
GATE_LORA = 128
RWKV_SHIFT_WIDTH = 3 * RWKV_WIDTH + DECAY_LORA + ICLR_LORA + GATE_LORA
IN_WIDTH = QKV_WIDTH + RWKV_SHIFT_WIDTH
D_FF = 4 * D_MODEL
RMS_EPS = 1e-6
GN_EPS = 64e-5
L2_EPS = 1e-12
NEG_INF = -1e30
LOG2E = math.log2(math.e)
Q_SCALE = HEAD_DIM ** -0.5 * LOG2E

LANES = 128
MXU_WIDTH = 256
PAIR = 2 * HEAD_DIM
N_PAIRS = RWKV_WIDTH // PAIR
CHUNK = 64
VMEM_LIMIT = 56 * 1024 * 1024

TM_PROJ = 512
TM_RWKV = 1024
TM_MLP = 512
TF_MLP = 512

NT_DIMS = (((1,), (1,)), ((), ()))
TN_DIMS = (((0,), (0,)), ((), ()))


def _dot(a, b):
    return jnp.dot(a, b, preferred_element_type=F32)


def _dot_nt(a, b):
    return lax.dot_general(a, b, NT_DIMS, preferred_element_type=F32)


def _dot_tn(a, b):
    return lax.dot_general(a, b, TN_DIMS, preferred_element_type=F32)


def _split_bf16(x, terms):
    parts = []
    rem = x
    for _ in range(terms - 1):
        p = rem.astype(BF16)
        parts.append(p)
        rem = rem - p.astype(F32)
    parts.append(rem.astype(BF16))
    return parts


def _dot_split_lhs(x, rhs_bf16, terms):
    acc = None
    for p in _split_bf16(x, terms):
        d = _dot(p, rhs_bf16)
        acc = d if acc is None else acc + d
    return acc


def _dot_split_rhs(lhs_bf16, x, terms):
    acc = None
    for p in _split_bf16(x, terms):
        d = _dot(lhs_bf16, p)
        acc = d if acc is None else acc + d
    return acc


def _sigmoid(x):
    return 1.0 / (1.0 + jnp.exp(-x))


def _rms_norm(x, g):
    ms = jnp.mean(x * x, axis=-1, keepdims=True)
    return x * lax.rsqrt(ms + RMS_EPS) * g


PROJ_TILE = 2 * MXU_WIDTH


def _proj_attn_kernel(sinks_ref, x_ref, g_ref, w_ref, mu_ref, zs_ref, o_ref,
                      qkv_ref, kvprev_ref, carry_ref, bias_ref, kvar_ref, vvar_ref, *, n_tiles, tiles_per_seq):
    t = pl.program_id(0)
    tile = jnp.minimum(t, n_tiles - 1)

    @pl.when(t == 0)
    def _():
        qkv_ref[...] = jnp.zeros_like(qkv_ref)
        kvprev_ref[...] = jnp.zeros_like(kvprev_ref)
        carry_ref[...] = jnp.zeros_like(carry_ref)
        qi = lax.broadcasted_iota(jnp.int32, (WINDOW, 2 * WINDOW), 0)
        kj = lax.broadcasted_iota(jnp.int32, (WINDOW, 2 * WINDOW), 1)
        dist = qi - kj + WINDOW
        valid = (dist >= 0) & (dist < WINDOW)
        distf = dist.astype(F32)
        for h in range(N_ATTN_HEADS):
            slope = 2.0 ** (-8.0 * (h + 1) / N_ATTN_HEADS)
            bias_ref[h] = jnp.where(valid, (-slope * LOG2E) * distf, NEG_INF)

    lane = lax.broadcasted_iota(jnp.int32, (1, LANES), 1)
    low = lane < HEAD_DIM

    kcol = slice(ATTN_WIDTH, ATTN_WIDTH + KV_WIDTH)
    vcol = slice(ATTN_WIDTH + KV_WIDTH, QKV_WIDTH)
    k_all = jnp.concatenate([kvprev_ref[:, :KV_WIDTH], qkv_ref[:, kcol]], axis=0)
    v_all = jnp.concatenate([kvprev_ref[:, KV_WIDTH:], qkv_ref[:, vcol]], axis=0)
    zero = jnp.zeros_like(k_all)
    k_sw = pltpu.roll(k_all.astype(F32), HEAD_DIM, axis=1).astype(BF16)
    v_sw = pltpu.roll(v_all.astype(F32), HEAD_DIM, axis=1).astype(BF16)
    lo_f = jnp.broadcast_to(jnp.where(low, 1.0, 0.0), k_all.shape)
    ones_lo = lo_f.astype(BF16)
    ones_hi = (1.0 - lo_f).astype(BF16)
    for g, (kg, vg) in enumerate([((k_all, k_sw), (v_all, v_sw)), ((k_sw, k_all), (v_sw, v_all))]):
        kvar_ref[2 * g] = jnp.where(low, kg[0], zero)
        kvar_ref[2 * g + 1] = jnp.where(low, zero, kg[1])
        vvar_ref[2 * g] = jnp.concatenate([jnp.where(low, vg[0], zero), ones_lo], axis=1)
        vvar_ref[2 * g + 1] = jnp.concatenate([jnp.where(low, zero, vg[1]), ones_hi], axis=1)

    opens_seq = lax.rem(t + jnp.int32(tiles_per_seq - 1), jnp.int32(tiles_per_seq)) == 0
    pen = jnp.where(opens_seq, NEG_INF, 0.0).astype(F32)

    def attn_block(j, pair):
        r0 = j * WINDOW
        qp = qkv_ref[r0:r0 + WINDOW, pair * PAIR:(pair + 1) * PAIR]
        g = (2 * pair) // GQA_GROUP
        ms, es = [], []
        for o in range(2):
            h = 2 * pair + o
            s = _dot_nt(qp, kvar_ref[2 * g + o, r0:r0 + 2 * WINDOW, :]) + bias_ref[h]
            if j == 0:
                s = jnp.concatenate([s[:, :WINDOW] + pen, s[:, WINDOW:]], axis=1)
            sink = sinks_ref[h] * LOG2E
            m = jnp.maximum(jnp.max(s, axis=-1, keepdims=True), sink)
            es.append(jnp.exp2(s - m).astype(BF16))
            ms.append(jnp.exp2(sink - m))
        acc = (_dot(es[0], vvar_ref[2 * g, r0:r0 + 2 * WINDOW, :])
               + _dot(es[1], vvar_ref[2 * g + 1, r0:r0 + 2 * WINDOW, :]))
        den = acc[:, PAIR:] + jnp.where(low, ms[0], ms[1])
        o_ref[r0:r0 + WINDOW, pair * PAIR:(pair + 1) * PAIR] = (acc[:, :PAIR] / den).astype(BF16)

    blocks = [(j, pair) for j in range(TM_PROJ // WINDOW) for pair in range(N_ATTN_HEADS // 2)]

    row0 = lax.broadcasted_iota(jnp.int32, (TM_PROJ, 1), 0) == 0
    opens_cur = lax.rem(tile, jnp.int32(tiles_per_seq)) == 0
    slot = lax.rem(tile, jnp.int32(2))

    hx = _rms_norm(x_ref[...], g_ref[...]).astype(BF16)
    n_pieces = IN_WIDTH // PROJ_TILE
    new_qkv = []
    done = 0
    for i in range(n_pieces):
        c0 = i * PROJ_TILE
        z = _dot(hx, w_ref[:, c0:c0 + PROJ_TILE])
        for c in range(c0, c0 + PROJ_TILE, MXU_WIDTH):
            zc = z[:, c - c0:c - c0 + MXU_WIDTH]
            if c < ATTN_WIDTH:
                new_qkv.append((zc * Q_SCALE).astype(BF16))
            elif c < QKV_WIDTH:
                new_qkv.append(zc.astype(BF16))
            else:
                cols = slice(c - QKV_WIDTH, c - QKV_WIDTH + MXU_WIDTH)
                last = jnp.where(opens_cur, 0.0, carry_ref[1 - slot, 0:1, cols])
                zprev = jnp.where(row0, last, pltpu.roll(zc, 1, axis=0))
                carry_ref[slot, 0:1, cols] = zc[TM_PROJ - 1:TM_PROJ, :]
                zs_ref[:, cols] = (zc + (zprev - zc) * mu_ref[:, cols]).astype(BF16)
        upto = len(blocks) * (i + 1) // n_pieces
        for blk in blocks[done:upto]:
            attn_block(*blk)
        done = upto

    kvprev_ref[...] = qkv_ref[TM_PROJ - WINDOW:, ATTN_WIDTH:]
    qkv_ref[...] = jnp.concatenate(new_qkv, axis=1)


def _proj_attn(x2, g, w_in, mu, sinks, tiles_per_seq):
    n = x2.shape[0]
    nt = n // TM_PROJ
    cur = lambda t: (jnp.minimum(t, nt - 1), 0)
    return pl.pallas_call(
        functools.partial(_proj_attn_kernel, n_tiles=nt, tiles_per_seq=tiles_per_seq),
        grid=(nt + 1,),
        in_specs=[
            pl.BlockSpec(memory_space=pltpu.SMEM),
            pl.BlockSpec((TM_PROJ, D_MODEL), cur),
            pl.BlockSpec((1, D_MODEL), lambda t: (0, 0)),
            pl.BlockSpec((D_MODEL, IN_WIDTH), lambda t: (0, 0)),
            pl.BlockSpec((1, RWKV_SHIFT_WIDTH), lambda t: (0, 0)),
        ],
        out_specs=[
            pl.BlockSpec((TM_PROJ, RWKV_SHIFT_WIDTH), cur),
            pl.BlockSpec((TM_PROJ, ATTN_WIDTH), lambda t: (jnp.maximum(t - 1, 0), 0)),
        ],
        out_shape=[
            jax.ShapeDtypeStruct((n, RWKV_SHIFT_WIDTH), BF16),
            jax.ShapeDtypeStruct((n, ATTN_WIDTH), BF16),
        ],
        scratch_shapes=[
            pltpu.VMEM((TM_PROJ, QKV_WIDTH), BF16),
            pltpu.VMEM((WINDOW, 2 * KV_WIDTH), BF16),
            pltpu.VMEM((2, 8, RWKV_SHIFT_WIDTH), F32),
            pltpu.VMEM((N_ATTN_HEADS, WINDOW, 2 * WINDOW), F32),
            pltpu.VMEM((2 * N_KV_HEADS, WINDOW + TM_PROJ, KV_WIDTH), BF16),
            pltpu.VMEM((2 * N_KV_HEADS, WINDOW + TM_PROJ, 2 * KV_WIDTH), BF16),
        ],
        compiler_params=pltpu.CompilerParams(
            dimension_semantics=("arbitrary",), vmem_limit_bytes=VMEM_LIMIT),
        name="proj_attn",
    )(sinks, x2, g, w_in, mu)


V_W0, V_A0, V_KK, V_KA, V_RK, V_LNW, V_LNB = range(7)
N_VECS = 8
GROUP = 4 * CHUNK
QUAD = MXU_WIDTH
N_QUADS = RWKV_WIDTH // QUAD
N_CHUNKS = TM_RWKV // CHUNK
OP_R, OP_A, OP_B, OP_K, OP_V = range(5)
N_HOOKS = 64
PREP_STAGES = 6
PREP_ROWS = 2 * CHUNK


def _rwkv_kernel(zs_ref, vecs_ref, wa_ref, g2_ref, ones_ref, o_ref,
                 state_ref, phi_ref, delta_ref, rw_ref, yc_ref, sin_ref, sout_ref,
                 bonus_ref, gate_ref, y_ref, ops_ref, dcol_ref):
    ti = pl.program_id(1)

    @pl.when(ti == 0)
    def _():
        state_ref[...] = jnp.zeros_like(state_ref)

    C = CHUNK
    G = GROUP
    lane = lax.broadcasted_iota(jnp.int32, (1, LANES), 1)
    low = lane < HEAD_DIM
    P = PREP_ROWS

    r2 = lax.broadcasted_iota(jnp.int32, (P, P), 0)
    c2 = lax.broadcasted_iota(jnp.int32, (P, P), 1)
    same_chunk = (r2 // C) == (c2 // C)
    rp = lax.broadcasted_iota(jnp.int32, (PAIR, PAIR), 0)
    cp = lax.broadcasted_iota(jnp.int32, (PAIR, PAIR), 1)
    same_head = (rp // HEAD_DIM) == (cp // HEAD_DIM)
    eye_p = rp == cp
    r1 = lax.broadcasted_iota(jnp.int32, (C, QUAD), 0)
    c1 = lax.broadcasted_iota(jnp.int32, (C, QUAD), 1) % C
    rs_strict = r1 > c1
    rs_incl = r1 >= c1
    eye_rs = (r1 == c1).astype(F32)
    head_of_lane = lax.broadcasted_iota(jnp.int32, (1, QUAD), 1) // HEAD_DIM
    mid = C // 2 - 1
    emat = jnp.where(same_chunk, (c2 <= r2).astype(F32) - ((c2 % C) <= mid).astype(F32), 0.0).astype(BF16)

    vec = lambda k: vecs_ref[k:k + 1, :]
    ones_bd = ones_ref[...]
    decay_scale = -math.exp(-0.5)

    def head_sum(x):
        w = ones_bd.shape[0]
        n = x.shape[0]
        hi_lo = jnp.concatenate(_split_bf16(x, 2), axis=0)
        sums = [_dot(hi_lo[:, o:o + w], ones_bd) for o in range(0, RWKV_WIDTH, w)]
        return jnp.concatenate([s[:n] + s[n:] for s in sums], axis=1)

    def stack4(x):
        zero = jnp.zeros_like(x)
        return jnp.concatenate(
            [jnp.where(head_of_lane == h, x, zero) for h in range(QUAD // HEAD_DIM)], axis=0)

    def prep(pi):
        r0 = pi * P
        zs = zs_ref[r0:r0 + P, :].astype(F32)
        xr = zs[:, 0:RWKV_WIDTH]
        xk = zs[:, RWKV_WIDTH:2 * RWKV_WIDTH]
        xv = zs[:, 2 * RWKV_WIDTH:3 * RWKV_WIDTH]
        xwa = zs[:, 3 * RWKV_WIDTH:3 * RWKV_WIDTH + LANES]
        xg = zs[:, 3 * RWKV_WIDTH + LANES:]
        yield

        lora = _dot(jnp.where(low, jnp.tanh(xwa), xwa).astype(BF16), wa_ref[...])
        gate_ref[r0:r0 + P, :] = _dot(_sigmoid(xg).astype(BF16), g2_ref[...])
        kk = xk * vec(V_KK)
        kk_norm2 = head_sum(kk * kk)
        yield

        lw = decay_scale * _sigmoid(vec(V_W0) + lora[:, :RWKV_WIDTH])
        a = _sigmoid(vec(V_A0) + lora[:, RWKV_WIDTH:])
        kk = kk / jnp.maximum(jnp.sqrt(kk_norm2), L2_EPS)
        k = xk * (1.0 + (a - 1.0) * vec(V_KA))
        yield

        bonus_ref[r0:r0 + P, :] = head_sum(xr * k * vec(V_RK)) * xv
        d1 = _dot_split_rhs(emat, lw, 2)
        yield

        e_neg = jnp.exp(-d1)
        ops_ref[OP_R, r0:r0 + P, :] = (xr * jnp.exp(d1)).astype(BF16)
        ops_ref[OP_A, r0:r0 + P, :] = (-kk * jnp.exp(d1 - lw)).astype(BF16)
        yield

        ops_ref[OP_B, r0:r0 + P, :] = (kk * a * e_neg).astype(BF16)
        ops_ref[OP_K, r0:r0 + P, :] = (k * e_neg).astype(BF16)
        ops_ref[OP_V, r0:r0 + P, :] = xv.astype(BF16)
        for j in range(P // C):
            ci = pi * (P // C) + j
            rho = lw[j * C:j * C + 1, :] - d1[j * C:j * C + 1, :]
            sin_ref[ci] = jnp.exp(rho)
            sout_ref[ci] = jnp.exp(d1[(j + 1) * C - 1:(j + 1) * C, :])

    def prep_work(pi):
        gen = prep(pi)
        return [lambda: next(gen, None)] * PREP_STAGES

    def chains(gi, hooks):
        chains = []
        for j in range(G // C):
            rows = slice(gi * G + j * C, gi * G + (j + 1) * C)
            for q in range(N_QUADS):
                sl = slice(q * QUAD, (q + 1) * QUAD)
                chains.append(dict(ci=gi * (G // C) + j, rows=rows, q=q, sl=sl,
                                   a=ops_ref[OP_A, rows, sl], r=ops_ref[OP_R, rows, sl], b=ops_ref[OP_B, rows, sl],
                                   k=ops_ref[OP_K, rows, sl], v=ops_ref[OP_V, rows, sl]))
        step = 0

        def run_hook():
            nonlocal step
            for item in hooks.get(step, []):
                item()
            step += 1

        for ch in chains:
            lg = jnp.concatenate([ch["a"], ch["r"]], axis=0)
            mf = _dot_nt(lg, jnp.concatenate([stack4(ch["b"]), stack4(ch["k"])], axis=0))
            ch["n"] = jnp.where(rs_strict, mf[:C, :QUAD], 0.0)
            ch["a_ak"] = jnp.where(rs_strict, mf[:C, QUAD:], 0.0).astype(BF16)
            ch["a_rb"] = jnp.where(rs_incl, mf[C:, :QUAD], 0.0).astype(BF16)
            ch["a_rk"] = jnp.where(rs_incl, mf[C:, QUAD:], 0.0).astype(BF16)
            ch["t"] = eye_rs + ch["n"]
            run_hook()
        for ch in chains:
            n_b = ch["n"].astype(BF16)
            ch["p"] = _dot(n_b, stack4(n_b)).astype(BF16)
            run_hook()
        pw = 2
        while pw < C:
            last = 2 * pw >= C
            for ch in chains:
                t_b = ch["t"].astype(BF16)
                lhs = t_b if last else jnp.concatenate([ch["p"], t_b], axis=0)
                res = _dot(lhs, stack4(ch["p"]))
                if last:
                    ch["t"] = ch["t"] + res
                else:
                    ch["p"] = res[:C].astype(BF16)
                    ch["t"] = ch["t"] + res[C:]
                run_hook()
            pw *= 2
        for ch in chains:
            akv = _dot(jnp.concatenate([ch["a_ak"], ch["a_rk"]], axis=0), stack4(ch["v"]))
            t_b = ch["t"].astype(BF16)
            art = _dot(ch["a_rb"], stack4(t_b)).astype(BF16)
            wu = _dot(jnp.concatenate([t_b, art], axis=0),
                      jnp.concatenate([stack4(ch["a"]), stack4(akv[:C].astype(BF16))], axis=1))
            ch["w"] = wu[:C, :QUAD]
            ch["u0"] = wu[:C, QUAD:].astype(BF16)
            ch["rw"] = ch["r"].astype(F32) + wu[C:, :QUAD]
            yc_ref[ch["rows"], ch["sl"]] = wu[C:, QUAD:] + akv[C:]
            run_hook()
        for ch in chains:
            s_in = sin_ref[ch["ci"]][:, ch["sl"]]
            s_out = sout_ref[ch["ci"]][:, ch["sl"]]
            rw_ref[ch["rows"], ch["sl"]] = (ch["rw"] * s_in).astype(BF16)
            w_in = (ch["w"] * s_in).astype(BF16)
            b_out = (ch["b"].astype(F32) * s_out).astype(BF16)
            k_out = (ch["k"].astype(F32) * s_out).astype(BF16)
            d_row = jnp.broadcast_to(s_in * s_out, (PAIR, QUAD))
            for h2 in range(QUAD // PAIR):
                ps = slice(h2 * PAIR, (h2 + 1) * PAIR)
                p = ch["q"] * (QUAD // PAIR) + h2
                phi_t = _dot_tn(b_out[:, ps], w_in[:, ps])
                dlt_t = _dot_tn(jnp.concatenate([b_out[:, ps], k_out[:, ps]], axis=0),
                                jnp.concatenate([ch["u0"][:, ps], ch["v"][:, ps]], axis=0))
                phi_ref[ch["ci"], p] = jnp.where(same_head, phi_t, 0.0).astype(BF16)
                delta_ref[ch["ci"], p] = jnp.where(same_head, dlt_t, 0.0)
                dcol_ref[ch["ci"], p] = jnp.sum(jnp.where(eye_p, d_row[:, ps], 0.0), axis=1, keepdims=True)

    def state_step(c):
        rows = slice(c * C, (c + 1) * C)
        for p in range(N_PAIRS):
            sl = slice(p * PAIR, (p + 1) * PAIR)
            h = state_ref[p]
            res = _dot(jnp.concatenate([rw_ref[rows, sl], phi_ref[c, p]], axis=0), h.astype(BF16))
            y_ref[rows, sl] = res[:C] + yc_ref[rows, sl]
            state_ref[p] = dcol_ref[c, p] * h + res[C:] + delta_ref[c, p]

    def finish(gi):
        blk = slice(gi * G, (gi + 1) * G)
        y = y_ref[blk, :]
        m1 = head_sum(y) * (1.0 / HEAD_DIM)
        dy = y - m1
        var = head_sum(dy * dy) * (1.0 / HEAD_DIM)
        yn = dy * lax.rsqrt(var + GN_EPS) * vec(V_LNW) + vec(V_LNB)
        o_ref[blk, :] = ((yn + bonus_ref[blk, :]) * gate_ref[blk, :]).astype(BF16)

    def tail_work(gi):
        return [lambda c=c: state_step(c) for c in range(gi * (G // C), (gi + 1) * (G // C))] + [
            lambda: finish(gi)]

    n_groups = TM_RWKV // G
    per_group = G // P
    for pi in range(per_group):
        for item in prep_work(pi):
            item()
    for gi in range(n_groups):
        work = []
        if gi + 1 < n_groups:
            for pi in range((gi + 1) * per_group, (gi + 2) * per_group):
                work += prep_work(pi)
        if gi > 0:
            work += tail_work(gi - 1)
        hooks = {}
        for i, item in enumerate(work):
            hooks.setdefault(i * N_HOOKS // len(work), []).append(item)
        chains(gi, hooks)
    for item in tail_work(n_groups - 1):
        item()


def _rwkv(zs3, vecs, wa_bd, g2, ones_bd):
    b, t, _ = zs3.shape
    const = lambda shape: pl.BlockSpec(shape, lambda bi, i: (0,) * len(shape))
    return pl.pallas_call(
        _rwkv_kernel,
        grid=(b, t // TM_RWKV),
        in_specs=[
            pl.BlockSpec((None, TM_RWKV, RWKV_SHIFT_WIDTH), lambda bi, i: (bi, i, 0)),
            const((N_VECS, RWKV_WIDTH)),
            const((LANES, 2 * RWKV_WIDTH)),
            const((GATE_LORA, RWKV_WIDTH)),
            const((MXU_WIDTH, MXU_WIDTH)),
        ],
        out_specs=pl.BlockSpec((None, TM_RWKV, RWKV_WIDTH), lambda bi, i: (bi, i, 0)),
        out_shape=jax.ShapeDtypeStruct((b, t, RWKV_WIDTH), BF16),
        scratch_shapes=[
            pltpu.VMEM((N_PAIRS, PAIR, PAIR), F32),
            pltpu.VMEM((N_CHUNKS, N_PAIRS, PAIR, PAIR), BF16),
            pltpu.VMEM((N_CHUNKS, N_PAIRS, PAIR, PAIR), F32),
            pltpu.VMEM((TM_RWKV, RWKV_WIDTH), BF16),
            pltpu.VMEM((TM_RWKV, RWKV_WIDTH), F32),
            pltpu.VMEM((N_CHUNKS, 1, RWKV_WIDTH), F32),
            pltpu.VMEM((N_CHUNKS, 1, RWKV_WIDTH), F32),
            pltpu.VMEM((TM_RWKV, RWKV_WIDTH), F32),
            pltpu.VMEM((TM_RWKV, RWKV_WIDTH), F32),
            pltpu.VMEM((TM_RWKV, RWKV_WIDTH), F32),
            pltpu.VMEM((5, TM_RWKV, RWKV_WIDTH), BF16),
            pltpu.VMEM((N_CHUNKS, N_PAIRS, PAIR, 1), F32),
        ],
        compiler_params=pltpu.CompilerParams(
            dimension_semantics=("arbitrary", "arbitrary"), vmem_limit_bytes=VMEM_LIMIT),
        name="rwkv",
    )(zs3, vecs, wa_bd, g2, ones_bd)


def _mlp_kernel(x_ref, at_ref, rw_ref, wo_ref, g1_ref, wu_ref, wd_ref, g2_ref, o_ref, act_ref):
    x1 = (x_ref[...] + _dot(at_ref[...], wo_ref[:ATTN_WIDTH, :])
          + _dot(rw_ref[...], wo_ref[ATTN_WIDTH:, :]))
    h = _rms_norm(x1, g1_ref[...]).astype(BF16)
    for f in range(D_FF // TF_MLP):
        sl = slice(f * TF_MLP, (f + 1) * TF_MLP)
        up = jnp.maximum(_dot(h, wu_ref[:, sl]), 0.0)
        act_ref[:, sl] = (up * up).astype(BF16)
    x2 = x1 + _dot(act_ref[...], wd_ref[...])
    o_ref[...] = _rms_norm(x2, g2_ref[...])


def _mlp(x2, attn, rwkv, w_out, g_mlp, w_up, w_down, g_final):
    n = x2.shape[0]
    const = lambda shape: pl.BlockSpec(shape, lambda i: (0, 0), pipeline_mode=pl.Buffered(1))
    return pl.pallas_call(
        _mlp_kernel,
        grid=(n // TM_MLP,),
        in_specs=[
            pl.BlockSpec((TM_MLP, D_MODEL), lambda i: (i, 0)),
            pl.BlockSpec((TM_MLP, ATTN_WIDTH), lambda i: (i, 0)),
            pl.BlockSpec((TM_MLP, RWKV_WIDTH), lambda i: (i, 0)),
            const((D_MODEL, D_MODEL)),
            const((1, D_MODEL)),
            const((D_MODEL, D_FF)),
            const((D_FF, D_MODEL)),
            const((1, D_MODEL)),
        ],
        out_specs=pl.BlockSpec((TM_MLP, D_MODEL), lambda i: (i, 0)),
        out_shape=jax.ShapeDtypeStruct((n, D_MODEL), F32),
        scratch_shapes=[pltpu.VMEM((TM_MLP, D_FF), BF16)],
        compiler_params=pltpu.CompilerParams(
            dimension_semantics=("arbitrary",), vmem_limit_bytes=VMEM_LIMIT),
        name="mlp",
    )(x2, attn, rwkv, w_out, g_mlp, w_up, w_down, g_final)


def _layer(x, attn_norm_g, w_in, attn_sinks, rwkv_mu, w0, w2, a0, a2, g2, k_k, k_a, r_k,
           ln_x_w, ln_x_b, w_out, mlp_norm_g, w_up, w_down):
    b, t, d = x.shape
    x2 = x.reshape(b * t, d)
    row = lambda v: v.reshape(1, -1)

    assert t % TM_PROJ == 0 and t % TM_RWKV == 0
    zs, attn = _proj_attn(x2, row(attn_norm_g), w_in.astype(BF16), row(rwkv_mu), attn_sinks, t // TM_PROJ)

    zero = jnp.zeros((DECAY_LORA, RWKV_WIDTH), F32)
    wa_bd = jnp.concatenate(
        [jnp.concatenate([w2, zero], axis=1), jnp.concatenate([zero, a2], axis=1)], axis=0).astype(BF16)
    vecs = jnp.stack([w0, a0, k_k, k_a, r_k, ln_x_w, ln_x_b, jnp.zeros_like(w0)], axis=0)
    head_id = jnp.arange(MXU_WIDTH) // HEAD_DIM
    ones_bd = (head_id[:, None] == head_id[None, :]).astype(BF16)
    rwkv = _rwkv(zs.reshape(b, t, RWKV_SHIFT_WIDTH), vecs, wa_bd, g2.astype(BF16), ones_bd)

    return x2, attn.reshape(b * t, ATTN_WIDTH), rwkv.reshape(b * t, RWKV_WIDTH)


def kernel(x, attn_norm_g, w_in, attn_sinks, rwkv_mu, w0, w2, a0, a2, g2, k_k, k_a, r_k, ln_x_w, ln_x_b,
           w_out, mlp_norm_g, w_up, w_down, final_norm_g):
    assert attn_norm_g.shape[0] == 1, "one trunk layer"
    b, t, d = x.shape
    x2, attn, rwkv = _layer(x, attn_norm_g[0], w_in[0], attn_sinks[0], rwkv_mu[0], w0[0], w2[0], a0[0], a2[0],
                            g2[0], k_k[0], k_a[0], r_k[0], ln_x_w[0], ln_x_b[0], w_out[0], mlp_norm_g[0],
                            w_up[0], w_down[0])
    out = _mlp(x2, attn, rwkv, w_out[0].astype(BF16), mlp_norm_g[0].reshape(1, -1), w_up[0].astype(BF16),
               w_down[0].astype(BF16), final_norm_g.reshape(1, -1))
    return out.reshape(b, t, d)
```

```python
import functools
import math

import jax
import jax.numpy as jnp
from jax import lax
from jax.experimental import pallas as pl
from jax.experimental.pallas import tpu as pltpu

F32 = jnp.float32
BF16 = jnp.bfloat16

D_MODEL = 1024
HEAD_DIM = 64
N_ATTN_HEADS = 8
N_KV_HEADS = 2
GQA_GROUP = N_ATTN_HEADS // N_KV_HEADS
ATTN_WIDTH = N_ATTN_HEADS * HEAD_DIM
KV_WIDTH = N_KV_HEADS * HEAD_DIM
QKV_WIDTH = ATTN_WIDTH + 2 * KV_WIDTH
N_RWKV_HEADS = 8
RWKV_WIDTH = N_RWKV_HEADS * HEAD_DIM
WINDOW = 128
DECAY_LORA = 64
ICLR_LORA = 64
GATE_LORA = 128
RWKV_SHIFT_WIDTH = 3 * RWKV_WIDTH + DECAY_LORA + ICLR_LORA + GATE_LORA
IN_WIDTH = QKV_WIDTH + RWKV_SHIFT_WIDTH
D_FF = 4 * D_MODEL
RMS_EPS = 1e-6
GN_EPS = 64e-5
L2_EPS = 1e-12
NEG_INF = -1e30
LOG2E = math.log2(math.e)
Q_SCALE = HEAD_DIM ** -0.5 * LOG2E

LANES = 128
MXU_WIDTH = 256
PAIR = 2 * HEAD_DIM
N_PAIRS = RWKV_WIDTH // PAIR
CHUNK = 64
VMEM_LIMIT = 56 * 1024 * 1024

TM_PROJ = 512
TM_RWKV = 1024
TM_MLP = 512
TF_MLP = 512

NT_DIMS = (((1,), (1,)), ((), ()))
TN_DIMS = (((0,), (0,)), ((), ()))


def _dot(a, b):
    return jnp.dot(a, b, preferred_element_type=F32)


def _dot_nt(a, b):
    return lax.dot_general(a, b, NT_DIMS, preferred_element_type=F32)


def _dot_tn(a, b):
    return lax.dot_general(a, b, TN_DIMS, preferred_element_type=F32)


def _split_bf16(x, terms):
    parts = []
    rem = x
    for _ in range(terms - 1):
        p = rem.astype(BF16)
        parts.append(p)
        rem = rem - p.astype(F32)
    parts.append(rem.astype(BF16))
    return parts


def _dot_split_lhs(x, rhs_bf16, terms):
    acc = None
    for p in _split_bf16(x, terms):
        d = _dot(p, rhs_bf16)
        acc = d if acc is None else acc + d
    return acc


def _dot_split_rhs(lhs_bf16, x, terms):
    acc = None
    for p in _split_bf16(x, terms):
        d = _dot(lhs_bf16, p)
        acc = d if acc is None else acc + d
    return acc


def _sigmoid(x):
    return 1.0 / (1.0 + jnp.exp(-x))


def _rms_norm(x, g):
    ms = jnp.mean(x * x, axis=-1, keepdims=True)
    return x * lax.rsqrt(ms + RMS_EPS) * g


PROJ_TILE = 2 * MXU_WIDTH


def _proj_attn_kernel(sinks_ref, x_ref, g_ref, w_ref, mu_ref, zs_ref, o_ref,
                      qkv_ref, kvprev_ref, carry_ref, bias_ref, kvar_ref, vvar_ref, *, n_tiles, tiles_per_seq):
    t = pl.program_id(0)
    tile = jnp.minimum(t, n_tiles - 1)

    @pl.when(t == 0)
    def _():
        qkv_ref[...] = jnp.zeros_like(qkv_ref)
        kvprev_ref[...] = jnp.zeros_like(kvprev_ref)
        carry_ref[...] = jnp.zeros_like(carry_ref)
        qi = lax.broadcasted_iota(jnp.int32, (WINDOW, 2 * WINDOW), 0)
        kj = lax.broadcasted_iota(jnp.int32, (WINDOW, 2 * WINDOW), 1)
        dist = qi - kj + WINDOW
        valid = (dist >= 0) & (dist < WINDOW)
        distf = dist.astype(F32)
        for h in range(N_ATTN_HEADS):
            slope = 2.0 ** (-8.0 * (h + 1) / N_ATTN_HEADS)
            bias_ref[h] = jnp.where(valid, (-slope * LOG2E) * distf, NEG_INF)

    lane = lax.broadcasted_iota(jnp.int32, (1, LANES), 1)
    low = lane < HEAD_DIM

    kcol = slice(ATTN_WIDTH, ATTN_WIDTH + KV_WIDTH)
    vcol = slice(ATTN_WIDTH + KV_WIDTH, QKV_WIDTH)
    k_all = jnp.concatenate([kvprev_ref[:, :KV_WIDTH], qkv_ref[:, kcol]], axis=0)
    v_all = jnp.concatenate([kvprev_ref[:, KV_WIDTH:], qkv_ref[:, vcol]], axis=0)
    zero = jnp.zeros_like(k_all)
    k_sw = pltpu.roll(k_all.astype(F32), HEAD_DIM, axis=1).astype(BF16)
    v_sw = pltpu.roll(v_all.astype(F32), HEAD_DIM, axis=1).astype(BF16)
    lo_f = jnp.broadcast_to(jnp.where(low, 1.0, 0.0), k_all.shape)
    ones_lo = lo_f.astype(BF16)
    ones_hi = (1.0 - lo_f).astype(BF16)
    for g, (kg, vg) in enumerate([((k_all, k_sw), (v_all, v_sw)), ((k_sw, k_all), (v_sw, v_all))]):
        kvar_ref[2 * g] = jnp.where(low, kg[0], zero)
        kvar_ref[2 * g + 1] = jnp.where(low, zero, kg[1])
        vvar_ref[2 * g] = jnp.concatenate([jnp.where(low, vg[0], zero), ones_lo], axis=1)
        vvar_ref[2 * g + 1] = jnp.concatenate([jnp.where(low, zero, vg[1]), ones_hi], axis=1)

    opens_seq = lax.rem(t + jnp.int32(tiles_per_seq - 1), jnp.int32(tiles_per_seq)) == 0
    pen = jnp.where(opens_seq, NEG_INF, 0.0).astype(F32)

    def attn_block(j, pair):
        r0 = j * WINDOW
        qp = qkv_ref[r0:r0 + WINDOW, pair * PAIR:(pair + 1) * PAIR]
        g = (2 * pair) // GQA_GROUP
        ms, es = [], []
        for o in range(2):
            h = 2 * pair + o
            s = _dot_nt(qp, kvar_ref[2 * g + o, r0:r0 + 2 * WINDOW, :]) + bias_ref[h]
            if j == 0:
                s = jnp.concatenate([s[:, :WINDOW] + pen, s[:, WINDOW:]], axis=1)
            sink = sinks_ref[h] * LOG2E
            m = jnp.maximum(jnp.max(s, axis=-1, keepdims=True), sink)
            es.append(jnp.exp2(s - m).astype(BF16))
            ms.append(jnp.exp2(sink - m))
        acc = (_dot(es[0], vvar_ref[2 * g, r0:r0 + 2 * WINDOW, :])
               + _dot(es[1], vvar_ref[2 * g + 1, r0:r0 + 2 * WINDOW, :]))
        den = acc[:, PAIR:] + jnp.where(low, ms[0], ms[1])
        o_ref[r0:r0 + WINDOW, pair * PAIR:(pair + 1) * PAIR] = (acc[:, :PAIR] / den).astype(BF16)

    blocks = [(j, pair) for j in range(TM_PROJ // WINDOW) for pair in range(N_ATTN_HEADS // 2)]

    row0 = lax.broadcasted_iota(jnp.int32, (TM_PROJ, 1), 0) == 0
    opens_cur = lax.rem(tile, jnp.int32(tiles_per_seq)) == 0
    slot = lax.rem(tile, jnp.int32(2))

    hx = _rms_norm(x_ref[...], g_ref[...]).astype(BF16)
    n_pieces = IN_WIDTH // PROJ_TILE
    new_qkv = []
    done = 0
    for i in range(n_pieces):
        c0 = i * PROJ_TILE
        z = _dot(hx, w_ref[:, c0:c0 + PROJ_TILE])
        for c in range(c0, c0 + PROJ_TILE, MXU_WIDTH):
            zc = z[:, c - c0:c - c0 + MXU_WIDTH]
            if c < ATTN_WIDTH:
                new_qkv.append((zc * Q_SCALE).astype(BF16))
            elif c < QKV_WIDTH:
                new_qkv.append(zc.astype(BF16))
            else:
                cols = slice(c - QKV_WIDTH, c - QKV_WIDTH + MXU_WIDTH)
                last = jnp.where(opens_cur, 0.0, carry_ref[1 - slot, 0:1, cols])
                zprev = jnp.where(row0, last, pltpu.roll(zc, 1, axis=0))
                carry_ref[slot, 0:1, cols] = zc[TM_PROJ - 1:TM_PROJ, :]
                zs_ref[:, cols] = (zc + (zprev - zc) * mu_ref[:, cols]).astype(BF16)
        upto = len(blocks) * (i + 1) // n_pieces
        for blk in blocks[done:upto]:
            attn_block(*blk)
        done = upto

    kvprev_ref[...] = qkv_ref[TM_PROJ - WINDOW:, ATTN_WIDTH:]
    qkv_ref[...] = jnp.concatenate(new_qkv, axis=1)


def _proj_attn(x2, g, w_in, mu, sinks, tiles_per_seq):
    n = x2.shape[0]
    nt = n // TM_PROJ
    cur = lambda t: (jnp.minimum(t, nt - 1), 0)
    return pl.pallas_call(
        functools.partial(_proj_attn_kernel, n_tiles=nt, tiles_per_seq=tiles_per_seq),
        grid=(nt + 1,),
        in_specs=[
            pl.BlockSpec(memory_space=pltpu.SMEM),
            pl.BlockSpec((TM_PROJ, D_MODEL), cur),
            pl.BlockSpec((1, D_MODEL), lambda t: (0, 0)),
            pl.BlockSpec((D_MODEL, IN_WIDTH), lambda t: (0, 0)),
            pl.BlockSpec((1, RWKV_SHIFT_WIDTH), lambda t: (0, 0)),
        ],
        out_specs=[
            pl.BlockSpec((TM_PROJ, RWKV_SHIFT_WIDTH), cur),
            pl.BlockSpec((TM_PROJ, ATTN_WIDTH), lambda t: (jnp.maximum(t - 1, 0), 0)),
        ],
        out_shape=[
            jax.ShapeDtypeStruct((n, RWKV_SHIFT_WIDTH), BF16),
            jax.ShapeDtypeStruct((n, ATTN_WIDTH), BF16),
        ],
        scratch_shapes=[
            pltpu.VMEM((TM_PROJ, QKV_WIDTH), BF16),
            pltpu.VMEM((WINDOW, 2 * KV_WIDTH), BF16),
            pltpu.VMEM((2, 8, RWKV_SHIFT_WIDTH), F32),
            pltpu.VMEM((N_ATTN_HEADS, WINDOW, 2 * WINDOW), F32),
            pltpu.VMEM((2 * N_KV_HEADS, WINDOW + TM_PROJ, KV_WIDTH), BF16),
            pltpu.VMEM((2 * N_KV_HEADS, WINDOW + TM_PROJ, 2 * KV_WIDTH), BF16),
        ],
        compiler_params=pltpu.CompilerParams(
            dimension_semantics=("arbitrary",), vmem_limit_bytes=VMEM_LIMIT),
        name="proj_attn",
    )(sinks, x2, g, w_in, mu)


V_W0, V_A0, V_KK, V_KA, V_RK, V_LNW, V_LNB = range(7)
N_VECS = 8
GROUP = 4 * CHUNK
QUAD = MXU_WIDTH
N_QUADS = RWKV_WIDTH // QUAD
N_CHUNKS = TM_RWKV // CHUNK
OP_R, OP_A, OP_B, OP_K, OP_V = range(5)
N_HOOKS = 64
PREP_STAGES = 6
HEAD_SUM_TERMS = 1
PREP_ROWS = 2 * CHUNK


def _rwkv_kernel(zs_ref, vecs_ref, wa_ref, g2_ref, ones_ref, o_ref,
                 state_ref, phi_ref, delta_ref, rw_ref, yc_ref, sin_ref, sout_ref,
                 bonus_ref, gate_ref, y_ref, ops_ref, dcol_ref):
    ti = pl.program_id(1)

    @pl.when(ti == 0)
    def _():
        state_ref[...] = jnp.zeros_like(state_ref)

    C = CHUNK
    G = GROUP
    lane = lax.broadcasted_iota(jnp.int32, (1, LANES), 1)
    low = lane < HEAD_DIM
    P = PREP_ROWS

    r2 = lax.broadcasted_iota(jnp.int32, (P, P), 0)
    c2 = lax.broadcasted_iota(jnp.int32, (P, P), 1)
    same_chunk = (r2 // C) == (c2 // C)
    rp = lax.broadcasted_iota(jnp.int32, (PAIR, PAIR), 0)
    cp = lax.broadcasted_iota(jnp.int32, (PAIR, PAIR), 1)
    same_head = (rp // HEAD_DIM) == (cp // HEAD_DIM)
    eye_p = rp == cp
    r1 = lax.broadcasted_iota(jnp.int32, (C, QUAD), 0)
    c1 = lax.broadcasted_iota(jnp.int32, (C, QUAD), 1) % C
    rs_strict = r1 > c1
    rs_incl = r1 >= c1
    eye_rs = (r1 == c1).astype(F32)
    head_of_lane = lax.broadcasted_iota(jnp.int32, (1, QUAD), 1) // HEAD_DIM
    mid = C // 2 - 1
    emat = jnp.where(same_chunk, (c2 <= r2).astype(F32) - ((c2 % C) <= mid).astype(F32), 0.0).astype(BF16)

    vec = lambda k: vecs_ref[k:k + 1, :]
    ones_bd = ones_ref[...]
    decay_scale = -math.exp(-0.5)

    def head_sum(x, terms=HEAD_SUM_TERMS):
        w = ones_bd.shape[0]
        n = x.shape[0]
        parts = jnp.concatenate(_split_bf16(x, terms), axis=0)
        sums = [_dot(parts[:, o:o + w], ones_bd) for o in range(0, RWKV_WIDTH, w)]
        return jnp.concatenate([sum(s[i * n:(i + 1) * n] for i in range(terms)) for s in sums], axis=1)

    def stack4(x):
        zero = jnp.zeros_like(x)
        return jnp.concatenate(
            [jnp.where(head_of_lane == h, x, zero) for h in range(QUAD // HEAD_DIM)], axis=0)

    def prep(pi):
        r0 = pi * P
        zs = zs_ref[r0:r0 + P, :].astype(F32)
        xr = zs[:, 0:RWKV_WIDTH]
        xk = zs[:, RWKV_WIDTH:2 * RWKV_WIDTH]
        xv = zs[:, 2 * RWKV_WIDTH:3 * RWKV_WIDTH]
        xwa = zs[:, 3 * RWKV_WIDTH:3 * RWKV_WIDTH + LANES]
        xg = zs[:, 3 * RWKV_WIDTH + LANES:]
        yield

        lora = _dot(jnp.where(low, jnp.tanh(xwa), xwa).astype(BF16), wa_ref[...])
        gate_ref[r0:r0 + P, :] = _dot(_sigmoid(xg).astype(BF16), g2_ref[...])
        kk = xk * vec(V_KK)
        kk_norm2 = head_sum(kk * kk)
        yield

        lw = decay_scale * _sigmoid(vec(V_W0) + lora[:, :RWKV_WIDTH])
        a = _sigmoid(vec(V_A0) + lora[:, RWKV_WIDTH:])
        kk = kk / jnp.maximum(jnp.sqrt(kk_norm2), L2_EPS)
        k = xk * (1.0 + (a - 1.0) * vec(V_KA))
        yield

        bonus_ref[r0:r0 + P, :] = head_sum(xr * k * vec(V_RK)) * xv
        d1 = _dot_split_rhs(emat, lw, 2)
        yield

        e_neg = jnp.exp(-d1)
        ops_ref[OP_R, r0:r0 + P, :] = (xr * jnp.exp(d1)).astype(BF16)
        ops_ref[OP_A, r0:r0 + P, :] = (-kk * jnp.exp(d1 - lw)).astype(BF16)
        yield

        ops_ref[OP_B, r0:r0 + P, :] = (kk * a * e_neg).astype(BF16)
        ops_ref[OP_K, r0:r0 + P, :] = (k * e_neg).astype(BF16)
        ops_ref[OP_V, r0:r0 + P, :] = xv.astype(BF16)
        for j in range(P // C):
            ci = pi * (P // C) + j
            rho = lw[j * C:j * C + 1, :] - d1[j * C:j * C + 1, :]
            sin_ref[ci] = jnp.exp(rho)
            sout_ref[ci] = jnp.exp(d1[(j + 1) * C - 1:(j + 1) * C, :])

    def prep_work(pi):
        gen = prep(pi)
        return [lambda: next(gen, None)] * PREP_STAGES

    def chains(gi, hooks):
        chains = []
        for j in range(G // C):
            rows = slice(gi * G + j * C, gi * G + (j + 1) * C)
            for q in range(N_QUADS):
                sl = slice(q * QUAD, (q + 1) * QUAD)
                chains.append(dict(ci=gi * (G // C) + j, rows=rows, q=q, sl=sl,
                                   a=ops_ref[OP_A, rows, sl], r=ops_ref[OP_R, rows, sl], b=ops_ref[OP_B, rows, sl],
                                   k=ops_ref[OP_K, rows, sl], v=ops_ref[OP_V, rows, sl]))
        step = 0

        def run_hook():
            nonlocal step
            for item in hooks.get(step, []):
                item()
            step += 1

        for ch in chains:
            lg = jnp.concatenate([ch["a"], ch["r"]], axis=0)
            mf = _dot_nt(lg, jnp.concatenate([stack4(ch["b"]), stack4(ch["k"])], axis=0))
            ch["n"] = jnp.where(rs_strict, mf[:C, :QUAD], 0.0)
            ch["a_ak"] = jnp.where(rs_strict, mf[:C, QUAD:], 0.0).astype(BF16)
            ch["a_rb"] = jnp.where(rs_incl, mf[C:, :QUAD], 0.0).astype(BF16)
            ch["a_rk"] = jnp.where(rs_incl, mf[C:, QUAD:], 0.0).astype(BF16)
            ch["t"] = eye_rs + ch["n"]
            run_hook()
        for ch in chains:
            n_b = ch["n"].astype(BF16)
            ch["p"] = _dot(n_b, stack4(n_b)).astype(BF16)
            run_hook()
        pw = 2
        while pw < C:
            last = 2 * pw >= C
            for ch in chains:
                t_b = ch["t"].astype(BF16)
                lhs = t_b if last else jnp.concatenate([ch["p"], t_b], axis=0)
                res = _dot(lhs, stack4(ch["p"]))
                if last:
                    ch["t"] = ch["t"] + res
                else:
                    ch["p"] = res[:C].astype(BF16)
                    ch["t"] = ch["t"] + res[C:]
                run_hook()
            pw *= 2
        for ch in chains:
            akv = _dot(jnp.concatenate([ch["a_ak"], ch["a_rk"]], axis=0), stack4(ch["v"]))
            t_b = ch["t"].astype(BF16)
            art = _dot(ch["a_rb"], stack4(t_b)).astype(BF16)
            wu = _dot(jnp.concatenate([t_b, art], axis=0),
                      jnp.concatenate([stack4(ch["a"]), stack4(akv[:C].astype(BF16))], axis=1))
            ch["w"] = wu[:C, :QUAD]
            ch["u0"] = wu[:C, QUAD:].astype(BF16)
            ch["rw"] = ch["r"].astype(F32) + wu[C:, :QUAD]
            yc_ref[ch["rows"], ch["sl"]] = wu[C:, QUAD:] + akv[C:]
            run_hook()
        for ch in chains:
            s_in = sin_ref[ch["ci"]][:, ch["sl"]]
            s_out = sout_ref[ch["ci"]][:, ch["sl"]]
            rw_ref[ch["rows"], ch["sl"]] = (ch["rw"] * s_in).astype(BF16)
            w_in = (ch["w"] * s_in).astype(BF16)
            b_out = (ch["b"].astype(F32) * s_out).astype(BF16)
            k_out = (ch["k"].astype(F32) * s_out).astype(BF16)
            d_row = jnp.broadcast_to(s_in * s_out, (PAIR, QUAD))
            for h2 in range(QUAD // PAIR):
                ps = slice(h2 * PAIR, (h2 + 1) * PAIR)
                p = ch["q"] * (QUAD // PAIR) + h2
                bk = jnp.concatenate([b_out[:, ps], k_out[:, ps]], axis=0)
                wuv = jnp.concatenate(
                    [jnp.concatenate([w_in[:, ps], ch["u0"][:, ps]], axis=1),
                     jnp.concatenate([jnp.zeros_like(w_in[:, ps]), ch["v"][:, ps]], axis=1)], axis=0)
                pd = _dot_tn(bk, wuv)
                phi_ref[ch["ci"], p] = jnp.where(same_head, pd[:, :PAIR], 0.0).astype(BF16)
                delta_ref[ch["ci"], p] = jnp.where(same_head, pd[:, PAIR:], 0.0)
                dcol_ref[ch["ci"], p] = jnp.sum(jnp.where(eye_p, d_row[:, ps], 0.0), axis=1, keepdims=True)

    def state_step(c):
        rows = slice(c * C, (c + 1) * C)
        for p in range(N_PAIRS):
            sl = slice(p * PAIR, (p + 1) * PAIR)
            h = state_ref[p]
            res = _dot(jnp.concatenate([rw_ref[rows, sl], phi_ref[c, p]], axis=0), h.astype(BF16))
            y_ref[rows, sl] = res[:C] + yc_ref[rows, sl]
            state_ref[p] = dcol_ref[c, p] * h + res[C:] + delta_ref[c, p]

    def finish(gi):
        blk = slice(gi * G, (gi + 1) * G)
        y = y_ref[blk, :]
        m1 = head_sum(y) * (1.0 / HEAD_DIM)
        dy = y - m1
        var = head_sum(dy * dy) * (1.0 / HEAD_DIM)
        yn = dy * lax.rsqrt(var + GN_EPS) * vec(V_LNW) + vec(V_LNB)
        o_ref[blk, :] = ((yn + bonus_ref[blk, :]) * gate_ref[blk, :]).astype(BF16)

    def tail_work(gi):
        return [lambda c=c: state_step(c) for c in range(gi * (G // C), (gi + 1) * (G // C))] + [
            lambda: finish(gi)]

    n_groups = TM_RWKV // G
    per_group = G // P
    for pi in range(per_group):
        for item in prep_work(pi):
            item()
    for gi in range(n_groups):
        work = []
        if gi + 1 < n_groups:
            for pi in range((gi + 1) * per_group, (gi + 2) * per_group):
                work += prep_work(pi)
        if gi > 0:
            work += tail_work(gi - 1)
        hooks = {}
        for i, item in enumerate(work):
            hooks.setdefault(i * N_HOOKS // len(work), []).append(item)
        chains(gi, hooks)
    for item in tail_work(n_groups - 1):
        item()


def _rwkv(zs3, vecs, wa_bd, g2, ones_bd):
    b, t, _ = zs3.shape
    const = lambda shape: pl.BlockSpec(shape, lambda bi, i: (0,) * len(shape))
    return pl.pallas_call(
        _rwkv_kernel,
        grid=(b, t // TM_RWKV),
        in_specs=[
            pl.BlockSpec((None, TM_RWKV, RWKV_SHIFT_WIDTH), lambda bi, i: (bi, i, 0)),
            const((N_VECS, RWKV_WIDTH)),
            const((LANES, 2 * RWKV_WIDTH)),
            const((GATE_LORA, RWKV_WIDTH)),
            const((MXU_WIDTH, MXU_WIDTH)),
        ],
        out_specs=pl.BlockSpec((None, TM_RWKV, RWKV_WIDTH), lambda bi, i: (bi, i, 0)),
        out_shape=jax.ShapeDtypeStruct((b, t, RWKV_WIDTH), BF16),
        scratch_shapes=[
            pltpu.VMEM((N_PAIRS, PAIR, PAIR), F32),
            pltpu.VMEM((N_CHUNKS, N_PAIRS, PAIR, PAIR), BF16),
            pltpu.VMEM((N_CHUNKS, N_PAIRS, PAIR, PAIR), F32),
            pltpu.VMEM((TM_RWKV, RWKV_WIDTH), BF16),
            pltpu.VMEM((TM_RWKV, RWKV_WIDTH), F32),
            pltpu.VMEM((N_CHUNKS, 1, RWKV_WIDTH), F32),
            pltpu.VMEM((N_CHUNKS, 1, RWKV_WIDTH), F32),
            pltpu.VMEM((TM_RWKV, RWKV_WIDTH), F32),
            pltpu.VMEM((TM_RWKV, RWKV_WIDTH), F32),
            pltpu.VMEM((TM_RWKV, RWKV_WIDTH), F32),
            pltpu.VMEM((5, TM_RWKV, RWKV_WIDTH), BF16),
            pltpu.VMEM((N_CHUNKS, N_PAIRS, PAIR, 1), F32),
        ],
        compiler_params=pltpu.CompilerParams(
            dimension_semantics=("arbitrary", "arbitrary"), vmem_limit_bytes=VMEM_LIMIT),
        name="rwkv",
    )(zs3, vecs, wa_bd, g2, ones_bd)


def _mlp_kernel(x_ref, at_ref, rw_ref, wo_ref, g1_ref, wu_ref, wd_ref, g2_ref, o_ref, act_ref):
    half = TM_MLP // 2
    rows_a, rows_b = slice(0, half), slice(half, TM_MLP)
    n_f = D_FF // TF_MLP

    def out_proj(rows):
        return (x_ref[rows, :] + _dot(at_ref[rows, :], wo_ref[:ATTN_WIDTH, :])
                + _dot(rw_ref[rows, :], wo_ref[ATTN_WIDTH:, :]))

    def up_piece(rows, h, f):
        sl = slice(f * TF_MLP, (f + 1) * TF_MLP)
        up = jnp.maximum(_dot(h, wu_ref[:, sl]), 0.0)
        act_ref[rows, sl] = (up * up).astype(BF16)

    x1a = out_proj(rows_a)
    x1b = out_proj(rows_b)
    ha = _rms_norm(x1a, g1_ref[...]).astype(BF16)
    up_piece(rows_a, ha, 0)
    hb = _rms_norm(x1b, g1_ref[...]).astype(BF16)
    for f in range(1, n_f):
        up_piece(rows_a, ha, f)
    x2a = x1a + _dot(act_ref[rows_a, :], wd_ref[...])
    up_piece(rows_b, hb, 0)
    o_ref[rows_a, :] = _rms_norm(x2a, g2_ref[...])
    for f in range(1, n_f):
        up_piece(rows_b, hb, f)
    x2b = x1b + _dot(act_ref[rows_b, :], wd_ref[...])
    o_ref[rows_b, :] = _rms_norm(x2b, g2_ref[...])


def _mlp(x2, attn, rwkv, w_out, g_mlp, w_up, w_down, g_final):
    n = x2.shape[0]
    const = lambda shape: pl.BlockSpec(shape, lambda i: (0, 0), pipeline_mode=pl.Buffered(1))
    return pl.pallas_call(
        _mlp_kernel,
        grid=(n // TM_MLP,),
        in_specs=[
            pl.BlockSpec((TM_MLP, D_MODEL), lambda i: (i, 0)),
            pl.BlockSpec((TM_MLP, ATTN_WIDTH), lambda i: (i, 0)),
            pl.BlockSpec((TM_MLP, RWKV_WIDTH), lambda i: (i, 0)),
            const((D_MODEL, D_MODEL)),
            const((1, D_MODEL)),
            const((D_MODEL, D_FF)),
            const((D_FF, D_MODEL)),
            const((1, D_MODEL)),
        ],
        out_specs=pl.BlockSpec((TM_MLP, D_MODEL), lambda i: (i, 0)),
        out_shape=jax.ShapeDtypeStruct((n, D_MODEL), F32),
        scratch_shapes=[pltpu.VMEM((TM_MLP, D_FF), BF16)],
        compiler_params=pltpu.CompilerParams(
            dimension_semantics=("arbitrary",), vmem_limit_bytes=VMEM_LIMIT),
        name="mlp",
    )(x2, attn, rwkv, w_out, g_mlp, w_up, w_down, g_final)


def _layer(x, attn_norm_g, w_in, attn_sinks, rwkv_mu, w0, w2, a0, a2, g2, k_k, k_a, r_k,
           ln_x_w, ln_x_b, w_out, mlp_norm_g, w_up, w_down):
    b, t, d = x.shape
    x2 = x.reshape(b * t, d)
    row = lambda v: v.reshape(1, -1)

    assert t % TM_PROJ == 0 and t % TM_RWKV == 0
    zs, attn = _proj_attn(x2, row(attn_norm_g), w_in.astype(BF16), row(rwkv_mu), attn_sinks, t // TM_PROJ)

    zero = jnp.zeros((DECAY_LORA, RWKV_WIDTH), F32)
    wa_bd = jnp.concatenate(
        [jnp.concatenate([w2, zero], axis=1), jnp.concatenate([zero, a2], axis=1)], axis=0).astype(BF16)
    vecs = jnp.stack([w0, a0, k_k, k_a, r_k, ln_x_w, ln_x_b, jnp.zeros_like(w0)], axis=0)
    head_id = jnp.arange(MXU_WIDTH) // HEAD_DIM
    ones_bd = (head_id[:, None] == head_id[None, :]).astype(BF16)
    rwkv = _rwkv(zs.reshape(b, t, RWKV_SHIFT_WIDTH), vecs, wa_bd, g2.astype(BF16), ones_bd)

    return x2, attn.reshape(b * t, ATTN_WIDTH), rwkv.reshape(b * t, RWKV_WIDTH)


def kernel(x, attn_norm_g, w_in, attn_sinks, rwkv_mu, w0, w2, a0, a2, g2, k_k, k_a, r_k, ln_x_w, ln_x_b,
           w_out, mlp_norm_g, w_up, w_down, final_norm_g):
    assert attn_norm_g.shape[0] == 1, "one trunk layer"
    b, t, d = x.shape
    x2, attn, rwkv = _layer(x, attn_norm_g[0], w_in[0], attn_sinks[0], rwkv_mu[0], w0[0], w2[0], a0[0], a2[0],
                            g2[0], k_k[0], k_a[0], r_k[0], ln_x_w[0], ln_x_b[0], w_out[0], mlp_norm_g[0],
                            w_up[0], w_down[0])
    out = _mlp(x2, attn, rwkv, w_out[0].astype(BF16), mlp_norm_g[0].reshape(1, -1), w_up[0].astype(BF16),
               w_down[0].astype(BF16), final_norm_g.reshape(1, -1))
    return out.reshape(b, t, d)
```

```python
import functools
import math

import jax
import jax.numpy as jnp
from jax import lax
from jax.experimental import pallas as pl
from jax.experimental.pallas import tpu as pltpu

F32 = jnp.float32
BF16 = jnp.bfloat16

D_MODEL = 1024
HEAD_DIM = 64
N_ATTN_HEADS = 8
N_KV_HEADS = 2
GQA_GROUP = N_ATTN_HEADS // N_KV_HEADS
ATTN_WIDTH = N_ATTN_HEADS * HEAD_DIM
KV_WIDTH = N_KV_HEADS * HEAD_DIM
QKV_WIDTH = ATTN_WIDTH + 2 * KV_WIDTH
N_RWKV_HEADS = 8
RWKV_WIDTH = N_RWKV_HEADS * HEAD_DIM
WINDOW = 128
DECAY_LORA = 64
ICLR_LORA = 64
GATE_LORA = 128
RWKV_SHIFT_WIDTH = 3 * RWKV_WIDTH + DECAY_LORA + ICLR_LORA + GATE_LORA
IN_WIDTH = QKV_WIDTH + RWKV_SHIFT_WIDTH
D_FF = 4 * D_MODEL
RMS_EPS = 1e-6
GN_EPS = 64e-5
L2_EPS = 1e-12
NEG_INF = -1e30
LOG2E = math.log2(math.e)
Q_SCALE = HEAD_DIM ** -0.5 * LOG2E

LANES = 128
MXU_WIDTH = 256
PAIR = 2 * HEAD_DIM
N_PAIRS = RWKV_WIDTH // PAIR
CHUNK = 64
VMEM_LIMIT = 56 * 1024 * 1024

TM_PROJ = 512
TM_RWKV = 1024
TM_MLP = 512
TF_MLP = 512

NT_DIMS = (((1,), (1,)), ((), ()))
TN_DIMS = (((0,), (0,)), ((), ()))


def _dot(a, b):
    return jnp.dot(a, b, preferred_element_type=F32)


def _dot_nt(a, b):
    return lax.dot_general(a, b, NT_DIMS, preferred_element_type=F32)


def _dot_tn(a, b):
    return lax.dot_general(a, b, TN_DIMS, preferred_element_type=F32)


def _split_bf16(x, terms):
    parts = []
    rem = x
    for _ in range(terms - 1):
        p = rem.astype(BF16)
        parts.append(p)
        rem = rem - p.astype(F32)
    parts.append(rem.astype(BF16))
    return parts


def _dot_split_lhs(x, rhs_bf16, terms):
    acc = None
    for p in _split_bf16(x, terms):
        d = _dot(p, rhs_bf16)
        acc = d if acc is None else acc + d
    return acc


def _dot_split_rhs(lhs_bf16, x, terms):
    acc = None
    for p in _split_bf16(x, terms):
        d = _dot(lhs_bf16, p)
        acc = d if acc is None else acc + d
    return acc


def _sigmoid(x):
    return 0.5 * jnp.tanh(0.5 * x) + 0.5


def _rms_norm(x, g):
    ms = jnp.mean(x * x, axis=-1, keepdims=True)
    return x * lax.rsqrt(ms + RMS_EPS) * g


PROJ_TILE = 2 * MXU_WIDTH


def _proj_attn_kernel(sinks_ref, x_ref, g_ref, w_ref, mu_ref, zs_ref, o_ref,
                      qkv_ref, kvprev_ref, carry_ref, bias_ref, kvar_ref, vvar_ref, *, n_tiles, tiles_per_seq):
    t = pl.program_id(0)
    tile = jnp.minimum(t, n_tiles - 1)

    @pl.when(t == 0)
    def _():
        qkv_ref[...] = jnp.zeros_like(qkv_ref)
        kvprev_ref[...] = jnp.zeros_like(kvprev_ref)
        carry_ref[...] = jnp.zeros_like(carry_ref)
        qi = lax.broadcasted_iota(jnp.int32, (WINDOW, 2 * WINDOW), 0)
        kj = lax.broadcasted_iota(jnp.int32, (WINDOW, 2 * WINDOW), 1)
        dist = qi - kj + WINDOW
        valid = (dist >= 0) & (dist < WINDOW)
        distf = dist.astype(F32)
        for h in range(N_ATTN_HEADS):
            slope = 2.0 ** (-8.0 * (h + 1) / N_ATTN_HEADS)
            bias_ref[h] = jnp.where(valid, (-slope * LOG2E) * distf, NEG_INF)

    lane = lax.broadcasted_iota(jnp.int32, (1, LANES), 1)
    low = lane < HEAD_DIM

    kcol = slice(ATTN_WIDTH, ATTN_WIDTH + KV_WIDTH)
    vcol = slice(ATTN_WIDTH + KV_WIDTH, QKV_WIDTH)
    k_all = jnp.concatenate([kvprev_ref[:, :KV_WIDTH], qkv_ref[:, kcol]], axis=0)
    v_all = jnp.concatenate([kvprev_ref[:, KV_WIDTH:], qkv_ref[:, vcol]], axis=0)
    zero = jnp.zeros_like(k_all)
    k_sw = pltpu.roll(k_all.astype(F32), HEAD_DIM, axis=1).astype(BF16)
    v_sw = pltpu.roll(v_all.astype(F32), HEAD_DIM, axis=1).astype(BF16)
    lo_f = jnp.broadcast_to(jnp.where(low, 1.0, 0.0), k_all.shape)
    ones_lo = lo_f.astype(BF16)
    ones_hi = (1.0 - lo_f).astype(BF16)
    for g, (kg, vg) in enumerate([((k_all, k_sw), (v_all, v_sw)), ((k_sw, k_all), (v_sw, v_all))]):
        kvar_ref[2 * g] = jnp.where(low, kg[0], zero)
        kvar_ref[2 * g + 1] = jnp.where(low, zero, kg[1])
        vvar_ref[2 * g] = jnp.concatenate([jnp.where(low, vg[0], zero), ones_lo], axis=1)
        vvar_ref[2 * g + 1] = jnp.concatenate([jnp.where(low, zero, vg[1]), ones_hi], axis=1)

    opens_seq = lax.rem(t + jnp.int32(tiles_per_seq - 1), jnp.int32(tiles_per_seq)) == 0
    pen = jnp.where(opens_seq, NEG_INF, 0.0).astype(F32)

    def attn_block(j, pair):
        r0 = j * WINDOW
        qp = qkv_ref[r0:r0 + WINDOW, pair * PAIR:(pair + 1) * PAIR]
        g = (2 * pair) // GQA_GROUP
        ms, es = [], []
        for o in range(2):
            h = 2 * pair + o
            s = _dot_nt(qp, kvar_ref[2 * g + o, r0:r0 + 2 * WINDOW, :]) + bias_ref[h]
            if j == 0:
                s = jnp.concatenate([s[:, :WINDOW] + pen, s[:, WINDOW:]], axis=1)
            sink = sinks_ref[h] * LOG2E
            m = jnp.maximum(jnp.max(s, axis=-1, keepdims=True), sink)
            es.append(jnp.exp2(s - m).astype(BF16))
            ms.append(jnp.exp2(sink - m))
        acc = (_dot(es[0], vvar_ref[2 * g, r0:r0 + 2 * WINDOW, :])
               + _dot(es[1], vvar_ref[2 * g + 1, r0:r0 + 2 * WINDOW, :]))
        den = acc[:, PAIR:] + jnp.where(low, ms[0], ms[1])
        o_ref[r0:r0 + WINDOW, pair * PAIR:(pair + 1) * PAIR] = (acc[:, :PAIR] / den).astype(BF16)

    blocks = [(j, pair) for j in range(TM_PROJ // WINDOW) for pair in range(N_ATTN_HEADS // 2)]

    row0 = lax.broadcasted_iota(jnp.int32, (TM_PROJ, 1), 0) == 0
    opens_cur = lax.rem(tile, jnp.int32(tiles_per_seq)) == 0
    slot = lax.rem(tile, jnp.int32(2))

    hx = _rms_norm(x_ref[...], g_ref[...]).astype(BF16)
    n_pieces = IN_WIDTH // PROJ_TILE
    new_qkv = []
    done = 0
    for i in range(n_pieces):
        c0 = i * PROJ_TILE
        z = _dot(hx, w_ref[:, c0:c0 + PROJ_TILE])
        for c in range(c0, c0 + PROJ_TILE, MXU_WIDTH):
            zc = z[:, c - c0:c - c0 + MXU_WIDTH]
            if c < ATTN_WIDTH:
                new_qkv.append((zc * Q_SCALE).astype(BF16))
            elif c < QKV_WIDTH:
                new_qkv.append(zc.astype(BF16))
            else:
                cols = slice(c - QKV_WIDTH, c - QKV_WIDTH + MXU_WIDTH)
                last = jnp.where(opens_cur, 0.0, carry_ref[1 - slot, 0:1, cols])
                zprev = jnp.where(row0, last, pltpu.roll(zc, 1, axis=0))
                carry_ref[slot, 0:1, cols] = zc[TM_PROJ - 1:TM_PROJ, :]
                zs_ref[:, cols] = (zc + (zprev - zc) * mu_ref[:, cols]).astype(BF16)
        upto = len(blocks) * (i + 1) // n_pieces
        for blk in blocks[done:upto]:
            attn_block(*blk)
        done = upto

    kvprev_ref[...] = qkv_ref[TM_PROJ - WINDOW:, ATTN_WIDTH:]
    qkv_ref[...] = jnp.concatenate(new_qkv, axis=1)


def _proj_attn(x2, g, w_in, mu, sinks, tiles_per_seq):
    n = x2.shape[0]
    nt = n // TM_PROJ
    cur = lambda t: (jnp.minimum(t, nt - 1), 0)
    return pl.pallas_call(
        functools.partial(_proj_attn_kernel, n_tiles=nt, tiles_per_seq=tiles_per_seq),
        grid=(nt + 1,),
        in_specs=[
            pl.BlockSpec(memory_space=pltpu.SMEM),
            pl.BlockSpec((TM_PROJ, D_MODEL), cur),
            pl.BlockSpec((1, D_MODEL), lambda t: (0, 0)),
            pl.BlockSpec((D_MODEL, IN_WIDTH), lambda t: (0, 0)),
            pl.BlockSpec((1, RWKV_SHIFT_WIDTH), lambda t: (0, 0)),
        ],
        out_specs=[
            pl.BlockSpec((TM_PROJ, RWKV_SHIFT_WIDTH), cur),
            pl.BlockSpec((TM_PROJ, ATTN_WIDTH), lambda t: (jnp.maximum(t - 1, 0), 0)),
        ],
        out_shape=[
            jax.ShapeDtypeStruct((n, RWKV_SHIFT_WIDTH), BF16),
            jax.ShapeDtypeStruct((n, ATTN_WIDTH), BF16),
        ],
        scratch_shapes=[
            pltpu.VMEM((TM_PROJ, QKV_WIDTH), BF16),
            pltpu.VMEM((WINDOW, 2 * KV_WIDTH), BF16),
            pltpu.VMEM((2, 8, RWKV_SHIFT_WIDTH), F32),
            pltpu.VMEM((N_ATTN_HEADS, WINDOW, 2 * WINDOW), F32),
            pltpu.VMEM((2 * N_KV_HEADS, WINDOW + TM_PROJ, KV_WIDTH), BF16),
            pltpu.VMEM((2 * N_KV_HEADS, WINDOW + TM_PROJ, 2 * KV_WIDTH), BF16),
        ],
        compiler_params=pltpu.CompilerParams(
            dimension_semantics=("arbitrary",), vmem_limit_bytes=VMEM_LIMIT),
        name="proj_attn",
    )(sinks, x2, g, w_in, mu)


V_W0, V_A0, V_KK, V_KA, V_RK, V_LNW, V_LNB = range(7)
N_VECS = 8
GROUP = 4 * CHUNK
QUAD = MXU_WIDTH
N_QUADS = RWKV_WIDTH // QUAD
N_CHUNKS = TM_RWKV // CHUNK
OP_R, OP_A, OP_B, OP_K, OP_V = range(5)
N_HOOKS_PER_CHAIN = 8
PREP_STAGES = 6
HEAD_SUM_TERMS = 1
PREP_ROWS = 2 * CHUNK


def _rwkv_kernel(zs_ref, vecs_ref, wa_ref, g2_ref, ones_ref, o_ref,
                 state_ref, phi_ref, delta_ref, rw_ref, yc_ref, sin_ref, sout_ref,
                 bonus_ref, gate_ref, y_ref, ops_ref, dcol_ref):
    ti = pl.program_id(1)

    @pl.when(ti == 0)
    def _():
        state_ref[...] = jnp.zeros_like(state_ref)

    C = CHUNK
    G = GROUP
    lane = lax.broadcasted_iota(jnp.int32, (1, LANES), 1)
    low = lane < HEAD_DIM
    P = PREP_ROWS

    r2 = lax.broadcasted_iota(jnp.int32, (P, P), 0)
    c2 = lax.broadcasted_iota(jnp.int32, (P, P), 1)
    same_chunk = (r2 // C) == (c2 // C)
    rp = lax.broadcasted_iota(jnp.int32, (PAIR, PAIR), 0)
    cp = lax.broadcasted_iota(jnp.int32, (PAIR, PAIR), 1)
    same_head = (rp // HEAD_DIM) == (cp // HEAD_DIM)
    eye_p = rp == cp
    r1 = lax.broadcasted_iota(jnp.int32, (C, QUAD), 0)
    c1 = lax.broadcasted_iota(jnp.int32, (C, QUAD), 1) % C
    rs_strict = r1 > c1
    rs_incl = r1 >= c1
    eye_rs = (r1 == c1).astype(F32)
    mid = C // 2 - 1
    emat = jnp.where(same_chunk, (c2 <= r2).astype(F32) - ((c2 % C) <= mid).astype(F32), 0.0).astype(BF16)

    vec = lambda k: vecs_ref[k:k + 1, :]
    ones_bd = ones_ref[...]
    decay_scale = -math.exp(-0.5)

    def head_sum(x, terms=HEAD_SUM_TERMS):
        w = ones_bd.shape[0]
        n = x.shape[0]
        parts = jnp.concatenate(_split_bf16(x, terms), axis=0)
        sums = [_dot(parts[:, o:o + w], ones_bd) for o in range(0, RWKV_WIDTH, w)]
        return jnp.concatenate([sum(s[i * n:(i + 1) * n] for i in range(terms)) for s in sums], axis=1)

    def stack4(x):
        halves = [x[:, :LANES], x[:, LANES:]]
        zero = jnp.zeros_like(halves[0])
        blocks = []
        for h in range(QUAD // HEAD_DIM):
            part = halves[h // 2]
            kept = jnp.where(low, part, zero) if h % 2 == 0 else jnp.where(low, zero, part)
            blocks.append(jnp.concatenate([kept, zero] if h < 2 else [zero, kept], axis=1))
        return jnp.concatenate(blocks, axis=0)

    def prep(pi):
        r0 = pi * P
        zs = zs_ref[r0:r0 + P, :].astype(F32)
        xr = zs[:, 0:RWKV_WIDTH]
        xk = zs[:, RWKV_WIDTH:2 * RWKV_WIDTH]
        xv = zs[:, 2 * RWKV_WIDTH:3 * RWKV_WIDTH]
        xwa = zs[:, 3 * RWKV_WIDTH:3 * RWKV_WIDTH + LANES]
        xg = zs[:, 3 * RWKV_WIDTH + LANES:]
        yield

        lora = _dot(jnp.where(low, jnp.tanh(xwa), xwa).astype(BF16), wa_ref[...])
        gate_ref[r0:r0 + P, :] = _dot(_sigmoid(xg).astype(BF16), g2_ref[...])
        kk = xk * vec(V_KK)
        kk_norm2 = head_sum(kk * kk)
        yield

        lw = decay_scale * _sigmoid(vec(V_W0) + lora[:, :RWKV_WIDTH])
        a = _sigmoid(vec(V_A0) + lora[:, RWKV_WIDTH:])
        kk = kk / jnp.maximum(jnp.sqrt(kk_norm2), L2_EPS)
        k = xk * (1.0 + (a - 1.0) * vec(V_KA))
        yield

        bonus_ref[r0:r0 + P, :] = head_sum(xr * k * vec(V_RK)) * xv
        d1 = _dot_split_rhs(emat, lw, 2)
        yield

        e_neg = jnp.exp(-d1)
        ops_ref[OP_R, r0:r0 + P, :] = (xr * jnp.exp(d1)).astype(BF16)
        ops_ref[OP_A, r0:r0 + P, :] = (-kk * jnp.exp(d1 - lw)).astype(BF16)
        yield

        ops_ref[OP_B, r0:r0 + P, :] = (kk * a * e_neg).astype(BF16)
        ops_ref[OP_K, r0:r0 + P, :] = (k * e_neg).astype(BF16)
        ops_ref[OP_V, r0:r0 + P, :] = xv.astype(BF16)
        for j in range(P // C):
            ci = pi * (P // C) + j
            rho = lw[j * C:j * C + 1, :] - d1[j * C:j * C + 1, :]
            sin_ref[ci] = jnp.exp(rho)
            sout_ref[ci] = jnp.exp(d1[(j + 1) * C - 1:(j + 1) * C, :])

    def prep_work(pi):
        gen = prep(pi)
        return [lambda: next(gen, None)] * PREP_STAGES

    def chains(grp, hooks):
        chains = []
        for ci in range(grp[0] // C, grp[1] // C):
            rows = slice(ci * C, (ci + 1) * C)
            for q in range(N_QUADS):
                sl = slice(q * QUAD, (q + 1) * QUAD)
                chains.append(dict(ci=ci, rows=rows, q=q, sl=sl,
                                   a=ops_ref[OP_A, rows, sl], r=ops_ref[OP_R, rows, sl], b=ops_ref[OP_B, rows, sl],
                                   k=ops_ref[OP_K, rows, sl], v=ops_ref[OP_V, rows, sl]))
        step = 0

        def run_hook():
            nonlocal step
            for item in hooks.get(step, []):
                item()
            step += 1

        for ch in chains:
            lg = jnp.concatenate([ch["a"], ch["r"]], axis=0)
            mf = _dot_nt(lg, jnp.concatenate([stack4(ch["b"]), stack4(ch["k"])], axis=0))
            ch["n"] = jnp.where(rs_strict, mf[:C, :QUAD], 0.0)
            ch["a_ak"] = jnp.where(rs_strict, mf[:C, QUAD:], 0.0).astype(BF16)
            ch["a_rb"] = jnp.where(rs_incl, mf[C:, :QUAD], 0.0).astype(BF16)
            ch["a_rk"] = jnp.where(rs_incl, mf[C:, QUAD:], 0.0).astype(BF16)
            ch["t"] = eye_rs + ch["n"]
            run_hook()
        for ch in chains:
            n_b = ch["n"].astype(BF16)
            ch["p"] = _dot(n_b, stack4(n_b)).astype(BF16)
            run_hook()
        pw = 2
        while pw < C:
            last = 2 * pw >= C
            for ch in chains:
                t_b = ch["t"].astype(BF16)
                lhs = t_b if last else jnp.concatenate([ch["p"], t_b], axis=0)
                res = _dot(lhs, stack4(ch["p"]))
                if last:
                    ch["t"] = ch["t"] + res
                else:
                    ch["p"] = res[:C].astype(BF16)
                    ch["t"] = ch["t"] + res[C:]
                run_hook()
            pw *= 2
        for ch in chains:
            akv = _dot(jnp.concatenate([ch["a_ak"], ch["a_rk"]], axis=0), stack4(ch["v"]))
            t_b = ch["t"].astype(BF16)
            art = _dot(ch["a_rb"], stack4(t_b)).astype(BF16)
            wu = _dot(jnp.concatenate([t_b, art], axis=0),
                      jnp.concatenate([stack4(ch["a"]), stack4(akv[:C].astype(BF16))], axis=1))
            ch["w"] = wu[:C, :QUAD]
            ch["u0"] = wu[:C, QUAD:].astype(BF16)
            ch["rw"] = ch["r"].astype(F32) + wu[C:, :QUAD]
            yc_ref[ch["rows"], ch["sl"]] = wu[C:, QUAD:] + akv[C:]
            run_hook()
        for ch in chains:
            s_in = sin_ref[ch["ci"]][:, ch["sl"]]
            s_out = sout_ref[ch["ci"]][:, ch["sl"]]
            rw_ref[ch["rows"], ch["sl"]] = (ch["rw"] * s_in).astype(BF16)
            w_in = (ch["w"] * s_in).astype(BF16)
            b_out = (ch["b"].astype(F32) * s_out).astype(BF16)
            k_out = (ch["k"].astype(F32) * s_out).astype(BF16)
            d_row = jnp.broadcast_to(s_in * s_out, (PAIR, QUAD))
            for h2 in range(QUAD // PAIR):
                ps = slice(h2 * PAIR, (h2 + 1) * PAIR)
                p = ch["q"] * (QUAD // PAIR) + h2
                bk = jnp.concatenate([b_out[:, ps], k_out[:, ps]], axis=0)
                wuv = jnp.concatenate(
                    [jnp.concatenate([w_in[:, ps], ch["u0"][:, ps]], axis=1),
                     jnp.concatenate([jnp.zeros_like(w_in[:, ps]), ch["v"][:, ps]], axis=1)], axis=0)
                pd = _dot_tn(bk, wuv)
                phi_ref[ch["ci"], p] = jnp.where(same_head, pd[:, :PAIR], 0.0).astype(BF16)
                delta_ref[ch["ci"], p] = jnp.where(same_head, pd[:, PAIR:], 0.0)
                dcol_ref[ch["ci"], p] = jnp.sum(jnp.where(eye_p, d_row[:, ps], 0.0), axis=1, keepdims=True)

    def state_step(c):
        rows = slice(c * C, (c + 1) * C)
        for p in range(N_PAIRS):
            sl = slice(p * PAIR, (p + 1) * PAIR)
            h = state_ref[p]
            res = _dot(jnp.concatenate([rw_ref[rows, sl], phi_ref[c, p]], axis=0), h.astype(BF16))
            y_ref[rows, sl] = res[:C] + yc_ref[rows, sl]
            state_ref[p] = dcol_ref[c, p] * h + res[C:] + delta_ref[c, p]

    def finish(grp):
        blk = slice(grp[0], grp[1])
        y = y_ref[blk, :]
        m1 = head_sum(y) * (1.0 / HEAD_DIM)
        dy = y - m1
        var = head_sum(dy * dy) * (1.0 / HEAD_DIM)
        yn = dy * lax.rsqrt(var + GN_EPS) * vec(V_LNW) + vec(V_LNB)
        o_ref[blk, :] = ((yn + bonus_ref[blk, :]) * gate_ref[blk, :]).astype(BF16)

    def tail_work(grp):
        return [lambda c=c: state_step(c) for c in range(grp[0] // C, grp[1] // C)] + [lambda: finish(grp)]

    bounds = list(range(0, TM_RWKV + 1, G))
    groups = list(zip(bounds[:-1], bounds[1:]))

    def prep_items(grp):
        return [item for pi in range(grp[0] // P, grp[1] // P) for item in prep_work(pi)]

    for item in prep_items(groups[0]):
        item()
    for gi, grp in enumerate(groups):
        work = []
        if gi + 1 < len(groups):
            work += prep_items(groups[gi + 1])
        if gi > 0:
            work += tail_work(groups[gi - 1])
        hooks = {}
        n_hooks = N_HOOKS_PER_CHAIN * (grp[1] - grp[0]) // C * N_QUADS
        for i, item in enumerate(work):
            hooks.setdefault(i * n_hooks // len(work), []).append(item)
        chains(grp, hooks)
    for item in tail_work(groups[-1]):
        item()


def _rwkv(zs3, vecs, wa_bd, g2, ones_bd):
    b, t, _ = zs3.shape
    const = lambda shape: pl.BlockSpec(shape, lambda bi, i: (0,) * len(shape))
    return pl.pallas_call(
        _rwkv_kernel,
        grid=(b, t // TM_RWKV),
        in_specs=[
            pl.BlockSpec((None, TM_RWKV, RWKV_SHIFT_WIDTH), lambda bi, i: (bi, i, 0)),
            const((N_VECS, RWKV_WIDTH)),
            const((LANES, 2 * RWKV_WIDTH)),
            const((GATE_LORA, RWKV_WIDTH)),
            const((MXU_WIDTH, MXU_WIDTH)),
        ],
        out_specs=pl.BlockSpec((None, TM_RWKV, RWKV_WIDTH), lambda bi, i: (bi, i, 0)),
        out_shape=jax.ShapeDtypeStruct((b, t, RWKV_WIDTH), BF16),
        scratch_shapes=[
            pltpu.VMEM((N_PAIRS, PAIR, PAIR), F32),
            pltpu.VMEM((N_CHUNKS, N_PAIRS, PAIR, PAIR), BF16),
            pltpu.VMEM((N_CHUNKS, N_PAIRS, PAIR, PAIR), F32),
            pltpu.VMEM((TM_RWKV, RWKV_WIDTH), BF16),
            pltpu.VMEM((TM_RWKV, RWKV_WIDTH), F32),
            pltpu.VMEM((N_CHUNKS, 1, RWKV_WIDTH), F32),
            pltpu.VMEM((N_CHUNKS, 1, RWKV_WIDTH), F32),
            pltpu.VMEM((TM_RWKV, RWKV_WIDTH), F32),
            pltpu.VMEM((TM_RWKV, RWKV_WIDTH), F32),
            pltpu.VMEM((TM_RWKV, RWKV_WIDTH), F32),
            pltpu.VMEM((5, TM_RWKV, RWKV_WIDTH), BF16),
            pltpu.VMEM((N_CHUNKS, N_PAIRS, PAIR, 1), F32),
        ],
        compiler_params=pltpu.CompilerParams(
            dimension_semantics=("arbitrary", "arbitrary"), vmem_limit_bytes=VMEM_LIMIT),
        name="rwkv",
    )(zs3, vecs, wa_bd, g2, ones_bd)


def _mlp_kernel(x_ref, at_ref, rw_ref, wo_ref, g1_ref, wu_ref, wd_ref, g2_ref, o_ref, act_ref):
    half = TM_MLP // 2
    rows_a, rows_b = slice(0, half), slice(half, TM_MLP)
    n_f = D_FF // TF_MLP

    def out_proj(rows):
        return (x_ref[rows, :] + _dot(at_ref[rows, :], wo_ref[:ATTN_WIDTH, :])
                + _dot(rw_ref[rows, :], wo_ref[ATTN_WIDTH:, :]))

    def up_piece(rows, h, f):
        sl = slice(f * TF_MLP, (f + 1) * TF_MLP)
        up = jnp.maximum(_dot(h, wu_ref[:, sl]), 0.0)
        act_ref[rows, sl] = (up * up).astype(BF16)

    x1a = out_proj(rows_a)
    x1b = out_proj(rows_b)
    ha = _rms_norm(x1a, g1_ref[...]).astype(BF16)
    up_piece(rows_a, ha, 0)
    hb = _rms_norm(x1b, g1_ref[...]).astype(BF16)
    for f in range(1, n_f):
        up_piece(rows_a, ha, f)
    x2a = x1a + _dot(act_ref[rows_a, :], wd_ref[...])
    up_piece(rows_b, hb, 0)
    o_ref[rows_a, :] = _rms_norm(x2a, g2_ref[...])
    for f in range(1, n_f):
        up_piece(rows_b, hb, f)
    x2b = x1b + _dot(act_ref[rows_b, :], wd_ref[...])
    o_ref[rows_b, :] = _rms_norm(x2b, g2_ref[...])


def _mlp(x2, attn, rwkv, w_out, g_mlp, w_up, w_down, g_final):
    n = x2.shape[0]
    const = lambda shape: pl.BlockSpec(shape, lambda i: (0, 0), pipeline_mode=pl.Buffered(1))
    return pl.pallas_call(
        _mlp_kernel,
        grid=(n // TM_MLP,),
        in_specs=[
            pl.BlockSpec((TM_MLP, D_MODEL), lambda i: (i, 0)),
            pl.BlockSpec((TM_MLP, ATTN_WIDTH), lambda i: (i, 0)),
            pl.BlockSpec((TM_MLP, RWKV_WIDTH), lambda i: (i, 0)),
            const((D_MODEL, D_MODEL)),
            const((1, D_MODEL)),
            const((D_MODEL, D_FF)),
            const((D_FF, D_MODEL)),
            const((1, D_MODEL)),
        ],
        out_specs=pl.BlockSpec((TM_MLP, D_MODEL), lambda i: (i, 0)),
        out_shape=jax.ShapeDtypeStruct((n, D_MODEL), F32),
        scratch_shapes=[pltpu.VMEM((TM_MLP, D_FF), BF16)],
        compiler_params=pltpu.CompilerParams(
            dimension_semantics=("arbitrary",), vmem_limit_bytes=VMEM_LIMIT),
        name="mlp",
    )(x2, attn, rwkv, w_out, g_mlp, w_up, w_down, g_final)


def _layer(x, attn_norm_g, w_in, attn_sinks, rwkv_mu, w0, w2, a0, a2, g2, k_k, k_a, r_k,
           ln_x_w, ln_x_b, w_out, mlp_norm_g, w_up, w_down):
    b, t, d = x.shape
    x2 = x.reshape(b * t, d)
    row = lambda v: v.reshape(1, -1)

    assert t % TM_PROJ == 0 and t % TM_RWKV == 0
    zs, attn = _proj_attn(x2, row(attn_norm_g), w_in.astype(BF16), row(rwkv_mu), attn_sinks, t // TM_PROJ)

    zero = jnp.zeros((DECAY_LORA, RWKV_WIDTH), F32)
    wa_bd = jnp.concatenate(
        [jnp.concatenate([w2, zero], axis=1), jnp.concatenate([zero, a2], axis=1)], axis=0).astype(BF16)
    vecs = jnp.stack([w0, a0, k_k, k_a, r_k, ln_x_w, ln_x_b, jnp.zeros_like(w0)], axis=0)
    head_id = jnp.arange(MXU_WIDTH) // HEAD_DIM
    ones_bd = (head_id[:, None] == head_id[None, :]).astype(BF16)
    rwkv = _rwkv(zs.reshape(b, t, RWKV_SHIFT_WIDTH), vecs, wa_bd, g2.astype(BF16), ones_bd)

    return x2, attn.reshape(b * t, ATTN_WIDTH), rwkv.reshape(b * t, RWKV_WIDTH)


def kernel(x, attn_norm_g, w_in, attn_sinks, rwkv_mu, w0, w2, a0, a2, g2, k_k, k_a, r_k, ln_x_w, ln_x_b,
           w_out, mlp_norm_g, w_up, w_down, final_norm_g):
    assert attn_norm_g.shape[0] == 1, "one trunk layer"
    b, t, d = x.shape
    x2, attn, rwkv = _layer(x, attn_norm_g[0], w_in[0], attn_sinks[0], rwkv_mu[0], w0[0], w2[0], a0[0], a2[0],
                            g2[0], k_k[0], k_a[0], r_k[0], ln_x_w[0], ln_x_b[0], w_out[0], mlp_norm_g[0],
                            w_up[0], w_down[0])
    out = _mlp(x2, attn, rwkv, w_out[0].astype(BF16), mlp_norm_g[0].reshape(1, -1), w_up[0].astype(BF16),
               w_down[0].astype(BF16), final_norm_g.reshape(1, -1))
    return out.reshape(b, t, d)
```

```python
import functools
import math

import jax
import jax.numpy as jnp
from jax import lax
from jax.experimental import pallas as pl
from jax.experimental.pallas import tpu as pltpu

F32 = jnp.float32
BF16 = jnp.bfloat16

D_MODEL = 1024
HEAD_DIM = 64
N_ATTN_HEADS = 8
N_KV_HEADS = 2
GQA_GROUP = N_ATTN_HEADS // N_KV_HEADS
ATTN_WIDTH = N_ATTN_HEADS * HEAD_DIM
KV_WIDTH = N_KV_HEADS * HEAD_DIM
QKV_WIDTH = ATTN_WIDTH + 2 * KV_WIDTH
N_RWKV_HEADS = 8
RWKV_WIDTH = N_RWKV_HEADS * HEAD_DIM
WINDOW = 128
DECAY_LORA = 64
ICLR_LORA = 64
GATE_LORA = 128
RWKV_SHIFT_WIDTH = 3 * RWKV_WIDTH + DECAY_LORA + ICLR_LORA + GATE_LORA
IN_WIDTH = QKV_WIDTH + RWKV_SHIFT_WIDTH
D_FF = 4 * D_MODEL
RMS_EPS = 1e-6
GN_EPS = 64e-5
L2_EPS = 1e-12
NEG_INF = -1e30
LOG2E = math.log2(math.e)
Q_SCALE = HEAD_DIM ** -0.5 * LOG2E

LANES = 128
MXU_WIDTH = 256
PAIR = 2 * HEAD_DIM
N_PAIRS = RWKV_WIDTH // PAIR
CHUNK = 64
VMEM_LIMIT = 56 * 1024 * 1024

TM_PROJ = 512
TM_RWKV = 1024
TM_MLP = 512
TF_MLP = 512

NT_DIMS = (((1,), (1,)), ((), ()))
TN_DIMS = (((0,), (0,)), ((), ()))


def _dot(a, b):
    return jnp.dot(a, b, preferred_element_type=F32)


def _dot_nt(a, b):
    return lax.dot_general(a, b, NT_DIMS, preferred_element_type=F32)


def _dot_tn(a, b):
    return lax.dot_general(a, b, TN_DIMS, preferred_element_type=F32)


def _split_bf16(x, terms):
    parts = []
    rem = x
    for _ in range(terms - 1):
        p = rem.astype(BF16)
        parts.append(p)
        rem = rem - p.astype(F32)
    parts.append(rem.astype(BF16))
    return parts


def _dot_split_rhs(lhs_bf16, x, terms):
    acc = None
    for p in _split_bf16(x, terms):
        d = _dot(lhs_bf16, p)
        acc = d if acc is None else acc + d
    return acc


def _sigmoid(x):
    return 0.5 * jnp.tanh(0.5 * x) + 0.5


def _rms_norm(x, g):
    ms = jnp.mean(x * x, axis=-1, keepdims=True)
    return x * lax.rsqrt(ms + RMS_EPS) * g


PROJ_TILE = 2 * MXU_WIDTH


def _proj_attn_kernel(sinks_ref, x_ref, g_ref, w_ref, mu_ref, zs_ref, o_ref,
                      qkv_ref, kvprev_ref, carry_ref, bias_ref, kvar_ref, vvar_ref, *, n_tiles, tiles_per_seq):
    t = pl.program_id(0)
    tile = jnp.minimum(t, n_tiles - 1)

    @pl.when(t == 0)
    def _():
        qkv_ref[...] = jnp.zeros_like(qkv_ref)
        kvprev_ref[...] = jnp.zeros_like(kvprev_ref)
        carry_ref[...] = jnp.zeros_like(carry_ref)
        qi = lax.broadcasted_iota(jnp.int32, (WINDOW, 2 * WINDOW), 0)
        kj = lax.broadcasted_iota(jnp.int32, (WINDOW, 2 * WINDOW), 1)
        dist = qi - kj + WINDOW
        valid = (dist >= 0) & (dist < WINDOW)
        distf = dist.astype(F32)
        for h in range(N_ATTN_HEADS):
            slope = 2.0 ** (-8.0 * (h + 1) / N_ATTN_HEADS)
            bias_ref[h] = jnp.where(valid, (-slope * LOG2E) * distf, NEG_INF)

    lane = lax.broadcasted_iota(jnp.int32, (1, LANES), 1)
    low = lane < HEAD_DIM

    kcol = slice(ATTN_WIDTH, ATTN_WIDTH + KV_WIDTH)
    vcol = slice(ATTN_WIDTH + KV_WIDTH, QKV_WIDTH)
    k_all = jnp.concatenate([kvprev_ref[:, :KV_WIDTH], qkv_ref[:, kcol]], axis=0)
    v_all = jnp.concatenate([kvprev_ref[:, KV_WIDTH:], qkv_ref[:, vcol]], axis=0)
    zero = jnp.zeros_like(k_all)
    k_sw = pltpu.roll(k_all.astype(F32), HEAD_DIM, axis=1).astype(BF16)
    v_sw = pltpu.roll(v_all.astype(F32), HEAD_DIM, axis=1).astype(BF16)
    lo_f = jnp.broadcast_to(jnp.where(low, 1.0, 0.0), k_all.shape)
    ones_lo = lo_f.astype(BF16)
    ones_hi = (1.0 - lo_f).astype(BF16)
    for g, (kg, vg) in enumerate([((k_all, k_sw), (v_all, v_sw)), ((k_sw, k_all), (v_sw, v_all))]):
        kvar_ref[2 * g] = jnp.where(low, kg[0], zero)
        kvar_ref[2 * g + 1] = jnp.where(low, zero, kg[1])
        vvar_ref[2 * g] = jnp.concatenate([jnp.where(low, vg[0], zero), ones_lo], axis=1)
        vvar_ref[2 * g + 1] = jnp.concatenate([jnp.where(low, zero, vg[1]), ones_hi], axis=1)

    opens_seq = lax.rem(t + jnp.int32(tiles_per_seq - 1), jnp.int32(tiles_per_seq)) == 0
    pen = jnp.where(opens_seq, NEG_INF, 0.0).astype(F32)

    def attn_block(j, pair):
        r0 = j * WINDOW
        qp = qkv_ref[r0:r0 + WINDOW, pair * PAIR:(pair + 1) * PAIR]
        g = (2 * pair) // GQA_GROUP
        ms, es = [], []
        for o in range(2):
            h = 2 * pair + o
            s = _dot_nt(qp, kvar_ref[2 * g + o, r0:r0 + 2 * WINDOW, :]) + bias_ref[h]
            if j == 0:
                s = jnp.concatenate([s[:, :WINDOW] + pen, s[:, WINDOW:]], axis=1)
            sink = sinks_ref[h] * LOG2E
            m = jnp.maximum(jnp.max(s, axis=-1, keepdims=True), sink)
            es.append(jnp.exp2(s - m).astype(BF16))
            ms.append(jnp.exp2(sink - m))
        acc = (_dot(es[0], vvar_ref[2 * g, r0:r0 + 2 * WINDOW, :])
               + _dot(es[1], vvar_ref[2 * g + 1, r0:r0 + 2 * WINDOW, :]))
        den = acc[:, PAIR:] + jnp.where(low, ms[0], ms[1])
        o_ref[r0:r0 + WINDOW, pair * PAIR:(pair + 1) * PAIR] = (acc[:, :PAIR] / den).astype(BF16)

    blocks = [(j, pair) for j in range(TM_PROJ // WINDOW) for pair in range(N_ATTN_HEADS // 2)]

    row0 = lax.broadcasted_iota(jnp.int32, (TM_PROJ, 1), 0) == 0
    opens_cur = lax.rem(tile, jnp.int32(tiles_per_seq)) == 0
    slot = lax.rem(tile, jnp.int32(2))

    hx = _rms_norm(x_ref[...], g_ref[...]).astype(BF16)
    n_pieces = IN_WIDTH // PROJ_TILE
    new_qkv = []
    done = 0
    for i in range(n_pieces):
        c0 = i * PROJ_TILE
        z = _dot(hx, w_ref[:, c0:c0 + PROJ_TILE])
        for c in range(c0, c0 + PROJ_TILE, MXU_WIDTH):
            zc = z[:, c - c0:c - c0 + MXU_WIDTH]
            if c < ATTN_WIDTH:
                new_qkv.append((zc * Q_SCALE).astype(BF16))
            elif c < QKV_WIDTH:
                new_qkv.append(zc.astype(BF16))
            else:
                cols = slice(c - QKV_WIDTH, c - QKV_WIDTH + MXU_WIDTH)
                last = jnp.where(opens_cur, 0.0, carry_ref[1 - slot, 0:1, cols])
                zprev = jnp.where(row0, last, pltpu.roll(zc, 1, axis=0))
                carry_ref[slot, 0:1, cols] = zc[TM_PROJ - 1:TM_PROJ, :]
                zs_ref[:, cols] = (zc + (zprev - zc) * mu_ref[:, cols]).astype(BF16)
        upto = len(blocks) * (i + 1) // n_pieces
        for blk in blocks[done:upto]:
            attn_block(*blk)
        done = upto

    kvprev_ref[...] = qkv_ref[TM_PROJ - WINDOW:, ATTN_WIDTH:]
    qkv_ref[...] = jnp.concatenate(new_qkv, axis=1)


def _proj_attn(x2, g, w_in, mu, sinks, tiles_per_seq):
    n = x2.shape[0]
    nt = n // TM_PROJ
    cur = lambda t: (jnp.minimum(t, nt - 1), 0)
    return pl.pallas_call(
        functools.partial(_proj_attn_kernel, n_tiles=nt, tiles_per_seq=tiles_per_seq),
        grid=(nt + 1,),
        in_specs=[
            pl.BlockSpec(memory_space=pltpu.SMEM),
            pl.BlockSpec((TM_PROJ, D_MODEL), cur),
            pl.BlockSpec((1, D_MODEL), lambda t: (0, 0)),
            pl.BlockSpec((D_MODEL, IN_WIDTH), lambda t: (0, 0)),
            pl.BlockSpec((1, RWKV_SHIFT_WIDTH), lambda t: (0, 0)),
        ],
        out_specs=[
            pl.BlockSpec((TM_PROJ, RWKV_SHIFT_WIDTH), cur),
            pl.BlockSpec((TM_PROJ, ATTN_WIDTH), lambda t: (jnp.maximum(t - 1, 0), 0)),
        ],
        out_shape=[
            jax.ShapeDtypeStruct((n, RWKV_SHIFT_WIDTH), BF16),
            jax.ShapeDtypeStruct((n, ATTN_WIDTH), BF16),
        ],
        scratch_shapes=[
            pltpu.VMEM((TM_PROJ, QKV_WIDTH), BF16),
            pltpu.VMEM((WINDOW, 2 * KV_WIDTH), BF16),
            pltpu.VMEM((2, 8, RWKV_SHIFT_WIDTH), F32),
            pltpu.VMEM((N_ATTN_HEADS, WINDOW, 2 * WINDOW), F32),
            pltpu.VMEM((2 * N_KV_HEADS, WINDOW + TM_PROJ, KV_WIDTH), BF16),
            pltpu.VMEM((2 * N_KV_HEADS, WINDOW + TM_PROJ, 2 * KV_WIDTH), BF16),
        ],
        compiler_params=pltpu.CompilerParams(
            dimension_semantics=("arbitrary",), vmem_limit_bytes=VMEM_LIMIT),
        name="proj_attn",
    )(sinks, x2, g, w_in, mu)


V_W0, V_A0, V_KK, V_KA, V_RK, V_LNW, V_LNB = range(7)
N_VECS = 8
GROUP = 4 * CHUNK
QUAD = MXU_WIDTH
N_QUADS = RWKV_WIDTH // QUAD
N_CHUNKS = TM_RWKV // CHUNK
assert CHUNK == HEAD_DIM, "the (C, 4C) layout puts one head's C x C matrix under that head's 64 lanes"
OP_R, OP_A, OP_B, OP_K, OP_V = range(5)
N_HOOKS_PER_CHAIN = 8
PREP_STAGES = 6
HEAD_SUM_TERMS = 1
PREP_ROWS = 2 * CHUNK


def _rwkv_kernel(zs_ref, vecs_ref, wa_ref, g2_ref, ones_ref, o_ref,
                 state_ref, phi_ref, delta_ref, rw_ref, yc_ref, sin_ref, sout_ref,
                 bonus_ref, gate_ref, y_ref, ops_ref, dcol_ref):
    ti = pl.program_id(1)

    @pl.when(ti == 0)
    def _():
        state_ref[...] = jnp.zeros_like(state_ref)

    C = CHUNK
    G = GROUP
    lane = lax.broadcasted_iota(jnp.int32, (1, LANES), 1)
    low = lane < HEAD_DIM
    P = PREP_ROWS

    r2 = lax.broadcasted_iota(jnp.int32, (P, P), 0)
    c2 = lax.broadcasted_iota(jnp.int32, (P, P), 1)
    same_chunk = (r2 // C) == (c2 // C)
    rp = lax.broadcasted_iota(jnp.int32, (PAIR, PAIR), 0)
    cp = lax.broadcasted_iota(jnp.int32, (PAIR, PAIR), 1)
    same_head = (rp // HEAD_DIM) == (cp // HEAD_DIM)
    eye_p = rp == cp
    r1 = lax.broadcasted_iota(jnp.int32, (C, QUAD), 0)
    c1 = lax.broadcasted_iota(jnp.int32, (C, QUAD), 1) % C
    rs_strict = r1 > c1
    rs_incl = r1 >= c1
    eye_rs = (r1 == c1).astype(F32)
    mid = C // 2 - 1
    emat = jnp.where(same_chunk, (c2 <= r2).astype(F32) - ((c2 % C) <= mid).astype(F32), 0.0).astype(BF16)

    vec = lambda k: vecs_ref[k:k + 1, :]
    ones_bd = ones_ref[...]
    decay_scale = -math.exp(-0.5)

    def head_sum(x, terms=HEAD_SUM_TERMS):
        w = ones_bd.shape[0]
        n = x.shape[0]
        parts = jnp.concatenate(_split_bf16(x, terms), axis=0)
        sums = [_dot(parts[:, o:o + w], ones_bd) for o in range(0, RWKV_WIDTH, w)]
        return jnp.concatenate([sum(s[i * n:(i + 1) * n] for i in range(terms)) for s in sums], axis=1)

    def stack4(x):
        halves = [x[:, :LANES], x[:, LANES:]]
        zero = jnp.zeros_like(halves[0])
        blocks = []
        for h in range(QUAD // HEAD_DIM):
            part = halves[h // 2]
            kept = jnp.where(low, part, zero) if h % 2 == 0 else jnp.where(low, zero, part)
            blocks.append(jnp.concatenate([kept, zero] if h < 2 else [zero, kept], axis=1))
        return jnp.concatenate(blocks, axis=0)

    def prep(pi):
        r0 = pi * P
        zs = zs_ref[r0:r0 + P, :].astype(F32)
        xr = zs[:, 0:RWKV_WIDTH]
        xk = zs[:, RWKV_WIDTH:2 * RWKV_WIDTH]
        xv = zs[:, 2 * RWKV_WIDTH:3 * RWKV_WIDTH]
        xwa = zs[:, 3 * RWKV_WIDTH:3 * RWKV_WIDTH + LANES]
        xg = zs[:, 3 * RWKV_WIDTH + LANES:]
        yield

        lora = _dot(jnp.where(low, jnp.tanh(xwa), xwa).astype(BF16), wa_ref[...])
        gate_ref[r0:r0 + P, :] = _dot(_sigmoid(xg).astype(BF16), g2_ref[...])
        kk = xk * vec(V_KK)
        kk_norm2 = head_sum(kk * kk)
        yield

        lw = decay_scale * _sigmoid(vec(V_W0) + lora[:, :RWKV_WIDTH])
        a = _sigmoid(vec(V_A0) + lora[:, RWKV_WIDTH:])
        kk = kk / jnp.maximum(jnp.sqrt(kk_norm2), L2_EPS)
        k = xk * (1.0 + (a - 1.0) * vec(V_KA))
        yield

        bonus_ref[r0:r0 + P, :] = head_sum(xr * k * vec(V_RK)) * xv
        d1 = _dot_split_rhs(emat, lw, 2)
        yield

        e_neg = jnp.exp(-d1)
        ops_ref[OP_R, r0:r0 + P, :] = (xr * jnp.exp(d1)).astype(BF16)
        ops_ref[OP_A, r0:r0 + P, :] = (-kk * jnp.exp(d1 - lw)).astype(BF16)
        yield

        ops_ref[OP_B, r0:r0 + P, :] = (kk * a * e_neg).astype(BF16)
        ops_ref[OP_K, r0:r0 + P, :] = (k * e_neg).astype(BF16)
        ops_ref[OP_V, r0:r0 + P, :] = xv.astype(BF16)
        for j in range(P // C):
            ci = pi * (P // C) + j
            rho = lw[j * C:j * C + 1, :] - d1[j * C:j * C + 1, :]
            sin_ref[ci] = jnp.exp(rho)
            sout_ref[ci] = jnp.exp(d1[(j + 1) * C - 1:(j + 1) * C, :])

    def prep_work(pi):
        gen = prep(pi)
        return [lambda: next(gen, None)] * PREP_STAGES

    def chains(grp, hooks):
        chains = []
        for ci in range(grp[0] // C, grp[1] // C):
            rows = slice(ci * C, (ci + 1) * C)
            for q in range(N_QUADS):
                sl = slice(q * QUAD, (q + 1) * QUAD)
                chains.append(dict(ci=ci, rows=rows, q=q, sl=sl,
                                   a=ops_ref[OP_A, rows, sl], r=ops_ref[OP_R, rows, sl], b=ops_ref[OP_B, rows, sl],
                                   k=ops_ref[OP_K, rows, sl], v=ops_ref[OP_V, rows, sl]))
        step = 0

        def run_hook():
            nonlocal step
            for item in hooks.get(step, []):
                item()
            step += 1

        for ch in chains:
            lg = jnp.concatenate([ch["a"], ch["r"]], axis=0)
            mf = _dot_nt(lg, jnp.concatenate([stack4(ch["b"]), stack4(ch["k"])], axis=0))
            ch["n"] = jnp.where(rs_strict, mf[:C, :QUAD], 0.0)
            ch["a_ak"] = jnp.where(rs_strict, mf[:C, QUAD:], 0.0).astype(BF16)
            ch["a_rb"] = jnp.where(rs_incl, mf[C:, :QUAD], 0.0).astype(BF16)
            ch["a_rk"] = jnp.where(rs_incl, mf[C:, QUAD:], 0.0).astype(BF16)
            ch["t"] = eye_rs + ch["n"]
            run_hook()
        for ch in chains:
            n_b = ch["n"].astype(BF16)
            ch["p"] = _dot(n_b, stack4(n_b)).astype(BF16)
            run_hook()
        pw = 2
        while pw < C:
            last = 2 * pw >= C
            for ch in chains:
                t_b = ch["t"].astype(BF16)
                lhs = t_b if last else jnp.concatenate([ch["p"], t_b], axis=0)
                res = _dot(lhs, stack4(ch["p"]))
                if last:
                    ch["t"] = ch["t"] + res
                else:
                    ch["p"] = res[:C].astype(BF16)
                    ch["t"] = ch["t"] + res[C:]
                run_hook()
            pw *= 2
        for ch in chains:
            akv = _dot(jnp.concatenate([ch["a_ak"], ch["a_rk"]], axis=0), stack4(ch["v"]))
            t_b = ch["t"].astype(BF16)
            art = _dot(ch["a_rb"], stack4(t_b)).astype(BF16)
            wu = _dot(jnp.concatenate([t_b, art], axis=0),
                      jnp.concatenate([stack4(ch["a"]), stack4(akv[:C].astype(BF16))], axis=1))
            ch["w"] = wu[:C, :QUAD]
            ch["u0"] = wu[:C, QUAD:].astype(BF16)
            ch["rw"] = ch["r"].astype(F32) + wu[C:, :QUAD]
            yc_ref[ch["rows"], ch["sl"]] = wu[C:, QUAD:] + akv[C:]
            run_hook()
        for ch in chains:
            s_in = sin_ref[ch["ci"]][:, ch["sl"]]
            s_out = sout_ref[ch["ci"]][:, ch["sl"]]
            rw_ref[ch["rows"], ch["sl"]] = (ch["rw"] * s_in).astype(BF16)
            w_in = (ch["w"] * s_in).astype(BF16)
            b_out = (ch["b"].astype(F32) * s_out).astype(BF16)
            k_out = (ch["k"].astype(F32) * s_out).astype(BF16)
            d_row = jnp.broadcast_to(s_in * s_out, (PAIR, QUAD))
            for h2 in range(QUAD // PAIR):
                ps = slice(h2 * PAIR, (h2 + 1) * PAIR)
                p = ch["q"] * (QUAD // PAIR) + h2
                bk = jnp.concatenate([b_out[:, ps], k_out[:, ps]], axis=0)
                wuv = jnp.concatenate(
                    [jnp.concatenate([w_in[:, ps], ch["u0"][:, ps]], axis=1),
                     jnp.concatenate([jnp.zeros_like(w_in[:, ps]), ch["v"][:, ps]], axis=1)], axis=0)
                pd = _dot_tn(bk, wuv)
                phi_ref[ch["ci"], p] = jnp.where(same_head, pd[:, :PAIR], 0.0).astype(BF16)
                delta_ref[ch["ci"], p] = jnp.where(same_head, pd[:, PAIR:], 0.0)
                dcol_ref[ch["ci"], p] = jnp.sum(jnp.where(eye_p, d_row[:, ps], 0.0), axis=1, keepdims=True)

    def state_step(c):
        rows = slice(c * C, (c + 1) * C)
        for p in range(N_PAIRS):
            sl = slice(p * PAIR, (p + 1) * PAIR)
            h = state_ref[p]
            res = _dot(jnp.concatenate([rw_ref[rows, sl], phi_ref[c, p]], axis=0), h.astype(BF16))
            y_ref[rows, sl] = res[:C] + yc_ref[rows, sl]
            state_ref[p] = dcol_ref[c, p] * h + res[C:] + delta_ref[c, p]

    def finish(grp):
        blk = slice(grp[0], grp[1])
        y = y_ref[blk, :]
        m1 = head_sum(y) * (1.0 / HEAD_DIM)
        dy = y - m1
        var = head_sum(dy * dy) * (1.0 / HEAD_DIM)
        yn = dy * lax.rsqrt(var + GN_EPS) * vec(V_LNW) + vec(V_LNB)
        o_ref[blk, :] = ((yn + bonus_ref[blk, :]) * gate_ref[blk, :]).astype(BF16)

    def tail_work(grp):
        return [lambda c=c: state_step(c) for c in range(grp[0] // C, grp[1] // C)] + [lambda: finish(grp)]

    bounds = list(range(0, TM_RWKV + 1, G))
    groups = list(zip(bounds[:-1], bounds[1:]))

    def prep_items(grp):
        return [item for pi in range(grp[0] // P, grp[1] // P) for item in prep_work(pi)]

    for item in prep_items(groups[0]):
        item()
    for gi, grp in enumerate(groups):
        work = []
        if gi + 1 < len(groups):
            work += prep_items(groups[gi + 1])
        if gi > 0:
            work += tail_work(groups[gi - 1])
        hooks = {}
        n_hooks = N_HOOKS_PER_CHAIN * (grp[1] - grp[0]) // C * N_QUADS
        for i, item in enumerate(work):
            hooks.setdefault(i * n_hooks // len(work), []).append(item)
        chains(grp, hooks)
    for item in tail_work(groups[-1]):
        item()


def _rwkv(zs3, vecs, wa_bd, g2, ones_bd):
    b, t, _ = zs3.shape
    const = lambda shape: pl.BlockSpec(shape, lambda bi, i: (0,) * len(shape))
    return pl.pallas_call(
        _rwkv_kernel,
        grid=(b, t // TM_RWKV),
        in_specs=[
            pl.BlockSpec((None, TM_RWKV, RWKV_SHIFT_WIDTH), lambda bi, i: (bi, i, 0)),
            const((N_VECS, RWKV_WIDTH)),
            const((LANES, 2 * RWKV_WIDTH)),
            const((GATE_LORA, RWKV_WIDTH)),
            const((MXU_WIDTH, MXU_WIDTH)),
        ],
        out_specs=pl.BlockSpec((None, TM_RWKV, RWKV_WIDTH), lambda bi, i: (bi, i, 0)),
        out_shape=jax.ShapeDtypeStruct((b, t, RWKV_WIDTH), BF16),
        scratch_shapes=[
            pltpu.VMEM((N_PAIRS, PAIR, PAIR), F32),
            pltpu.VMEM((N_CHUNKS, N_PAIRS, PAIR, PAIR), BF16),
            pltpu.VMEM((N_CHUNKS, N_PAIRS, PAIR, PAIR), F32),
            pltpu.VMEM((TM_RWKV, RWKV_WIDTH), BF16),
            pltpu.VMEM((TM_RWKV, RWKV_WIDTH), F32),
            pltpu.VMEM((N_CHUNKS, 1, RWKV_WIDTH), F32),
            pltpu.VMEM((N_CHUNKS, 1, RWKV_WIDTH), F32),
            pltpu.VMEM((TM_RWKV, RWKV_WIDTH), F32),
            pltpu.VMEM((TM_RWKV, RWKV_WIDTH), F32),
            pltpu.VMEM((TM_RWKV, RWKV_WIDTH), F32),
            pltpu.VMEM((5, TM_RWKV, RWKV_WIDTH), BF16),
            pltpu.VMEM((N_CHUNKS, N_PAIRS, PAIR, 1), F32),
        ],
        compiler_params=pltpu.CompilerParams(
            dimension_semantics=("arbitrary", "arbitrary"), vmem_limit_bytes=VMEM_LIMIT),
        name="rwkv",
    )(zs3, vecs, wa_bd, g2, ones_bd)


def _mlp_kernel(x_ref, at_ref, rw_ref, wo_ref, g1_ref, wu_ref, wd_ref, g2_ref, o_ref, act_ref):
    half = TM_MLP // 2
    rows_a, rows_b = slice(0, half), slice(half, TM_MLP)
    n_f = D_FF // TF_MLP

    def out_proj(rows):
        return (x_ref[rows, :] + _dot(at_ref[rows, :], wo_ref[:ATTN_WIDTH, :])
                + _dot(rw_ref[rows, :], wo_ref[ATTN_WIDTH:, :]))

    def up_piece(rows, h, f):
        sl = slice(f * TF_MLP, (f + 1) * TF_MLP)
        up = jnp.maximum(_dot(h, wu_ref[:, sl]), 0.0)
        act_ref[rows, sl] = (up * up).astype(BF16)

    x1a = out_proj(rows_a)
    x1b = out_proj(rows_b)
    ha = _rms_norm(x1a, g1_ref[...]).astype(BF16)
    up_piece(rows_a, ha, 0)
    hb = _rms_norm(x1b, g1_ref[...]).astype(BF16)
    for f in range(1, n_f):
        up_piece(rows_a, ha, f)
    x2a = x1a + _dot(act_ref[rows_a, :], wd_ref[...])
    up_piece(rows_b, hb, 0)
    o_ref[rows_a, :] = _rms_norm(x2a, g2_ref[...])
    for f in range(1, n_f):
        up_piece(rows_b, hb, f)
    x2b = x1b + _dot(act_ref[rows_b, :], wd_ref[...])
    o_ref[rows_b, :] = _rms_norm(x2b, g2_ref[...])


def _mlp(x2, attn, rwkv, w_out, g_mlp, w_up, w_down, g_final):
    n = x2.shape[0]
    const = lambda shape: pl.BlockSpec(shape, lambda i: (0, 0), pipeline_mode=pl.Buffered(1))
    return pl.pallas_call(
        _mlp_kernel,
        grid=(n // TM_MLP,),
        in_specs=[
            pl.BlockSpec((TM_MLP, D_MODEL), lambda i: (i, 0)),
            pl.BlockSpec((TM_MLP, ATTN_WIDTH), lambda i: (i, 0)),
            pl.BlockSpec((TM_MLP, RWKV_WIDTH), lambda i: (i, 0)),
            const((D_MODEL, D_MODEL)),
            const((1, D_MODEL)),
            const((D_MODEL, D_FF)),
            const((D_FF, D_MODEL)),
            const((1, D_MODEL)),
        ],
        out_specs=pl.BlockSpec((TM_MLP, D_MODEL), lambda i: (i, 0)),
        out_shape=jax.ShapeDtypeStruct((n, D_MODEL), F32),
        scratch_shapes=[pltpu.VMEM((TM_MLP, D_FF), BF16)],
        compiler_params=pltpu.CompilerParams(
            dimension_semantics=("arbitrary",), vmem_limit_bytes=VMEM_LIMIT),
        name="mlp",
    )(x2, attn, rwkv, w_out, g_mlp, w_up, w_down, g_final)


def _layer(x, attn_norm_g, w_in, attn_sinks, rwkv_mu, w0, w2, a0, a2, g2, k_k, k_a, r_k,
           ln_x_w, ln_x_b, w_out, mlp_norm_g, w_up, w_down):
    b, t, d = x.shape
    x2 = x.reshape(b * t, d)
    row = lambda v: v.reshape(1, -1)

    assert t % TM_PROJ == 0 and t % TM_RWKV == 0
    zs, attn = _proj_attn(x2, row(attn_norm_g), w_in.astype(BF16), row(rwkv_mu), attn_sinks, t // TM_PROJ)

    zero = jnp.zeros((DECAY_LORA, RWKV_WIDTH), F32)
    wa_bd = jnp.concatenate(
        [jnp.concatenate([w2, zero], axis=1), jnp.concatenate([zero, a2], axis=1)], axis=0).astype(BF16)
    vecs = jnp.stack([w0, a0, k_k, k_a, r_k, ln_x_w, ln_x_b, jnp.zeros_like(w0)], axis=0)
    head_id = jnp.arange(MXU_WIDTH) // HEAD_DIM
    ones_bd = (head_id[:, None] == head_id[None, :]).astype(BF16)
    rwkv = _rwkv(zs.reshape(b, t, RWKV_SHIFT_WIDTH), vecs, wa_bd, g2.astype(BF16), ones_bd)

    return x2, attn.reshape(b * t, ATTN_WIDTH), rwkv.reshape(b * t, RWKV_WIDTH)


def kernel(x, attn_norm_g, w_in, attn_sinks, rwkv_mu, w0, w2, a0, a2, g2, k_k, k_a, r_k, ln_x_w, ln_x_b,
           w_out, mlp_norm_g, w_up, w_down, final_norm_g):
    assert attn_norm_g.shape[0] == 1, "one trunk layer"
    b, t, d = x.shape
    x2, attn, rwkv = _layer(x, attn_norm_g[0], w_in[0], attn_sinks[0], rwkv_mu[0], w0[0], w2[0], a0[0], a2[0],
                            g2[0], k_k[0], k_a[0], r_k[0], ln_x_w[0], ln_x_b[0], w_out[0], mlp_norm_g[0],
                            w_up[0], w_down[0])
    out = _mlp(x2, attn, rwkv, w_out[0].astype(BF16), mlp_norm_g[0].reshape(1, -1), w_up[0].astype(BF16),
               w_down[0].astype(BF16), final_norm_g.reshape(1, -1))
    return out.reshape(b, t, d)
```

```python
import functools
import math

import jax
import jax.numpy as jnp
from jax import lax
from jax.experimental import pallas as pl
from jax.experimental.pallas import tpu as pltpu

F32 = jnp.float32
BF16 = jnp.bfloat16

D_MODEL = 1024
HEAD_DIM = 64
N_ATTN_HEADS = 8
N_KV_HEADS = 2
GQA_GROUP = N_ATTN_HEADS // N_KV_HEADS
ATTN_WIDTH = N_ATTN_HEADS * HEAD_DIM
KV_WIDTH = N_KV_HEADS * HEAD_DIM
QKV_WIDTH = ATTN_WIDTH + 2 * KV_WIDTH
N_RWKV_HEADS = 8
RWKV_WIDTH = N_RWKV_HEADS * HEAD_DIM
WINDOW = 128
DECAY_LORA = 64
ICLR_LORA = 64
GATE_LORA = 128
RWKV_SHIFT_WIDTH = 3 * RWKV_WIDTH + DECAY_LORA + ICLR_LORA + GATE_LORA
IN_WIDTH = QKV_WIDTH + RWKV_SHIFT_WIDTH
D_FF = 4 * D_MODEL
RMS_EPS = 1e-6
GN_EPS = 64e-5
L2_EPS = 1e-12
NEG_INF = -1e30
LOG2E = math.log2(math.e)
Q_SCALE = HEAD_DIM ** -0.5 * LOG2E

LANES = 128
MXU_WIDTH = 256
PAIR = 2 * HEAD_DIM
N_PAIRS = RWKV_WIDTH // PAIR
CHUNK = 64
VMEM_LIMIT = 56 * 1024 * 1024

TM_PROJ = 512
TM_RWKV = 1024
TM_MLP = 512
TF_MLP = 512

NT_DIMS = (((1,), (1,)), ((), ()))
TN_DIMS = (((0,), (0,)), ((), ()))


def _dot(a, b):
    return jnp.dot(a, b, preferred_element_type=F32)


def _dot_nt(a, b):
    return lax.dot_general(a, b, NT_DIMS, preferred_element_type=F32)


def _dot_tn(a, b):
    return lax.dot_general(a, b, TN_DIMS, preferred_element_type=F32)


def _split_bf16(x, terms):
    parts = []
    rem = x
    for _ in range(terms - 1):
        p = rem.astype(BF16)
        parts.append(p)
        rem = rem - p.astype(F32)
    parts.append(rem.astype(BF16))
    return parts


def _dot_split_rhs(lhs_bf16, x, terms):
    acc = None
    for p in _split_bf16(x, terms):
        d = _dot(lhs_bf16, p)
        acc = d if acc is None else acc + d
    return acc


def _sigmoid(x):
    return 0.5 * jnp.tanh(0.5 * x) + 0.5


def _rms_norm(x, g):
    ms = jnp.mean(x * x, axis=-1, keepdims=True)
    return x * lax.rsqrt(ms + RMS_EPS) * g


PROJ_TILE = 2 * MXU_WIDTH


def _proj_attn_kernel(sinks_ref, x_ref, g_ref, w_ref, mu_ref, zs_ref, o_ref,
                      qkv_ref, kvprev_ref, carry_ref, bias_ref, kvar_ref, vvar_ref, *, n_tiles, tiles_per_seq):
    t = pl.program_id(0)
    tile = jnp.minimum(t, n_tiles - 1)

    @pl.when(t == 0)
    def _():
        qkv_ref[...] = jnp.zeros_like(qkv_ref)
        kvprev_ref[...] = jnp.zeros_like(kvprev_ref)
        carry_ref[...] = jnp.zeros_like(carry_ref)
        qi = lax.broadcasted_iota(jnp.int32, (WINDOW, 2 * WINDOW), 0)
        kj = lax.broadcasted_iota(jnp.int32, (WINDOW, 2 * WINDOW), 1)
        dist = qi - kj + WINDOW
        valid = (dist >= 0) & (dist < WINDOW)
        distf = dist.astype(F32)
        for h in range(N_ATTN_HEADS):
            slope = 2.0 ** (-8.0 * (h + 1) / N_ATTN_HEADS)
            bias_ref[h] = jnp.where(valid, (-slope * LOG2E) * distf, NEG_INF)

    lane = lax.broadcasted_iota(jnp.int32, (1, LANES), 1)
    low = lane < HEAD_DIM

    kcol = slice(ATTN_WIDTH, ATTN_WIDTH + KV_WIDTH)
    vcol = slice(ATTN_WIDTH + KV_WIDTH, QKV_WIDTH)
    k_all = jnp.concatenate([kvprev_ref[:, :KV_WIDTH], qkv_ref[:, kcol]], axis=0)
    v_all = jnp.concatenate([kvprev_ref[:, KV_WIDTH:], qkv_ref[:, vcol]], axis=0)
    zero = jnp.zeros_like(k_all)
    k_sw = pltpu.roll(k_all.astype(F32), HEAD_DIM, axis=1).astype(BF16)
    v_sw = pltpu.roll(v_all.astype(F32), HEAD_DIM, axis=1).astype(BF16)
    lo_f = jnp.broadcast_to(jnp.where(low, 1.0, 0.0), k_all.shape)
    ones_lo = lo_f.astype(BF16)
    ones_hi = (1.0 - lo_f).astype(BF16)
    for g, (kg, vg) in enumerate([((k_all, k_sw), (v_all, v_sw)), ((k_sw, k_all), (v_sw, v_all))]):
        kvar_ref[2 * g] = jnp.where(low, kg[0], zero)
        kvar_ref[2 * g + 1] = jnp.where(low, zero, kg[1])
        vvar_ref[2 * g] = jnp.concatenate([jnp.where(low, vg[0], zero), ones_lo], axis=1)
        vvar_ref[2 * g + 1] = jnp.concatenate([jnp.where(low, zero, vg[1]), ones_hi], axis=1)

    opens_seq = lax.rem(t + jnp.int32(tiles_per_seq - 1), jnp.int32(tiles_per_seq)) == 0
    pen = jnp.where(opens_seq, NEG_INF, 0.0).astype(F32)

    def attn_block(j, pair):
        r0 = j * WINDOW
        qp = qkv_ref[r0:r0 + WINDOW, pair * PAIR:(pair + 1) * PAIR]
        g = (2 * pair) // GQA_GROUP
        ms, es = [], []
        for o in range(2):
            h = 2 * pair + o
            s = _dot_nt(qp, kvar_ref[2 * g + o, r0:r0 + 2 * WINDOW, :]) + bias_ref[h]
            if j == 0:
                s = jnp.concatenate([s[:, :WINDOW] + pen, s[:, WINDOW:]], axis=1)
            sink = sinks_ref[h] * LOG2E
            m = jnp.maximum(jnp.max(s, axis=-1, keepdims=True), sink)
            es.append(jnp.exp2(s - m).astype(BF16))
            ms.append(jnp.exp2(sink - m))
        acc = (_dot(es[0], vvar_ref[2 * g, r0:r0 + 2 * WINDOW, :])
               + _dot(es[1], vvar_ref[2 * g + 1, r0:r0 + 2 * WINDOW, :]))
        den = acc[:, PAIR:] + jnp.where(low, ms[0], ms[1])
        o_ref[r0:r0 + WINDOW, pair * PAIR:(pair + 1) * PAIR] = (acc[:, :PAIR] / den).astype(BF16)

    blocks = [(j, pair) for j in range(TM_PROJ // WINDOW) for pair in range(N_ATTN_HEADS // 2)]

    row0 = lax.broadcasted_iota(jnp.int32, (TM_PROJ, 1), 0) == 0
    opens_cur = lax.rem(tile, jnp.int32(tiles_per_seq)) == 0
    slot = lax.rem(tile, jnp.int32(2))

    hx = _rms_norm(x_ref[...], g_ref[...]).astype(BF16)
    n_pieces = IN_WIDTH // PROJ_TILE
    new_qkv = []
    done = 0
    for i in range(n_pieces):
        c0 = i * PROJ_TILE
        z = _dot(hx, w_ref[:, c0:c0 + PROJ_TILE])
        for c in range(c0, c0 + PROJ_TILE, MXU_WIDTH):
            zc = z[:, c - c0:c - c0 + MXU_WIDTH]
            if c < ATTN_WIDTH:
                new_qkv.append((zc * Q_SCALE).astype(BF16))
            elif c < QKV_WIDTH:
                new_qkv.append(zc.astype(BF16))
            else:
                cols = slice(c - QKV_WIDTH, c - QKV_WIDTH + MXU_WIDTH)
                last = jnp.where(opens_cur, 0.0, carry_ref[1 - slot, 0:1, cols])
                zprev = jnp.where(row0, last, pltpu.roll(zc, 1, axis=0))
                carry_ref[slot, 0:1, cols] = zc[TM_PROJ - 1:TM_PROJ, :]
                zs_ref[:, cols] = (zc + (zprev - zc) * mu_ref[:, cols]).astype(BF16)
        upto = len(blocks) * (i + 1) // n_pieces
        for blk in blocks[done:upto]:
            attn_block(*blk)
        done = upto

    kvprev_ref[...] = qkv_ref[TM_PROJ - WINDOW:, ATTN_WIDTH:]
    qkv_ref[...] = jnp.concatenate(new_qkv, axis=1)


def _proj_attn(x2, g, w_in, mu, sinks, tiles_per_seq):
    n = x2.shape[0]
    nt = n // TM_PROJ
    cur = lambda t: (jnp.minimum(t, nt - 1), 0)
    return pl.pallas_call(
        functools.partial(_proj_attn_kernel, n_tiles=nt, tiles_per_seq=tiles_per_seq),
        grid=(nt + 1,),
        in_specs=[
            pl.BlockSpec(memory_space=pltpu.SMEM),
            pl.BlockSpec((TM_PROJ, D_MODEL), cur),
            pl.BlockSpec((1, D_MODEL), lambda t: (0, 0)),
            pl.BlockSpec((D_MODEL, IN_WIDTH), lambda t: (0, 0)),
            pl.BlockSpec((1, RWKV_SHIFT_WIDTH), lambda t: (0, 0)),
        ],
        out_specs=[
            pl.BlockSpec((TM_PROJ, RWKV_SHIFT_WIDTH), cur),
            pl.BlockSpec((TM_PROJ, ATTN_WIDTH), lambda t: (jnp.maximum(t - 1, 0), 0)),
        ],
        out_shape=[
            jax.ShapeDtypeStruct((n, RWKV_SHIFT_WIDTH), BF16),
            jax.ShapeDtypeStruct((n, ATTN_WIDTH), BF16),
        ],
        scratch_shapes=[
            pltpu.VMEM((TM_PROJ, QKV_WIDTH), BF16),
            pltpu.VMEM((WINDOW, 2 * KV_WIDTH), BF16),
            pltpu.VMEM((2, 8, RWKV_SHIFT_WIDTH), F32),
            pltpu.VMEM((N_ATTN_HEADS, WINDOW, 2 * WINDOW), F32),
            pltpu.VMEM((2 * N_KV_HEADS, WINDOW + TM_PROJ, KV_WIDTH), BF16),
            pltpu.VMEM((2 * N_KV_HEADS, WINDOW + TM_PROJ, 2 * KV_WIDTH), BF16),
        ],
        compiler_params=pltpu.CompilerParams(
            dimension_semantics=("arbitrary",), vmem_limit_bytes=VMEM_LIMIT),
        name="proj_attn",
    )(sinks, x2, g, w_in, mu)


V_W0, V_A0, V_KK, V_KA, V_RK, V_LNW, V_LNB = range(7)
N_VECS = 8
GROUP = 4 * CHUNK
QUAD = MXU_WIDTH
N_QUADS = RWKV_WIDTH // QUAD
N_CHUNKS = TM_RWKV // CHUNK
assert CHUNK == HEAD_DIM, "the (C, 4C) layout puts one head's C x C matrix under that head's 64 lanes"
OP_R, OP_A, OP_B, OP_K, OP_V = range(5)
N_HOOKS_PER_CHAIN = 8
PREP_STAGES = 6
HEAD_SUM_TERMS = 1
PREP_ROWS = 2 * CHUNK


def _rwkv_kernel(zs_ref, vecs_ref, wa_ref, g2_ref, ones_ref, o_ref,
                 state_ref, phi_ref, delta_ref, rw_ref, yc_ref, sin_ref, sout_ref,
                 bonus_ref, gate_ref, y_ref, ops_ref, dcol_ref):
    ti = pl.program_id(1)

    @pl.when(ti == 0)
    def _():
        state_ref[...] = jnp.zeros_like(state_ref)

    C = CHUNK
    G = GROUP
    lane = lax.broadcasted_iota(jnp.int32, (1, LANES), 1)
    low = lane < HEAD_DIM
    P = PREP_ROWS

    r2 = lax.broadcasted_iota(jnp.int32, (P, P), 0)
    c2 = lax.broadcasted_iota(jnp.int32, (P, P), 1)
    same_chunk = (r2 // C) == (c2 // C)
    rp = lax.broadcasted_iota(jnp.int32, (PAIR, PAIR), 0)
    cp = lax.broadcasted_iota(jnp.int32, (PAIR, PAIR), 1)
    same_head = (rp // HEAD_DIM) == (cp // HEAD_DIM)
    eye_p = rp == cp
    r1 = lax.broadcasted_iota(jnp.int32, (C, QUAD), 0)
    c1 = lax.broadcasted_iota(jnp.int32, (C, QUAD), 1) % C
    rs_strict = r1 > c1
    rs_incl = r1 >= c1
    eye_rs = (r1 == c1).astype(F32)
    mid = C // 2 - 1
    emat = jnp.where(same_chunk, (c2 <= r2).astype(F32) - ((c2 % C) <= mid).astype(F32), 0.0).astype(BF16)

    vec = lambda k: vecs_ref[k:k + 1, :]
    ones_bd = ones_ref[...]
    decay_scale = -math.exp(-0.5)

    def head_sum(x, terms=HEAD_SUM_TERMS):
        w = ones_bd.shape[0]
        n = x.shape[0]
        parts = jnp.concatenate(_split_bf16(x, terms), axis=0)
        sums = [_dot(parts[:, o:o + w], ones_bd) for o in range(0, RWKV_WIDTH, w)]
        return jnp.concatenate([sum(s[i * n:(i + 1) * n] for i in range(terms)) for s in sums], axis=1)

    def stack4(x):
        halves = [x[:, :LANES], x[:, LANES:]]
        zero = jnp.zeros_like(halves[0])
        blocks = []
        for h in range(QUAD // HEAD_DIM):
            part = halves[h // 2]
            kept = jnp.where(low, part, zero) if h % 2 == 0 else jnp.where(low, zero, part)
            blocks.append(jnp.concatenate([kept, zero] if h < 2 else [zero, kept], axis=1))
        return jnp.concatenate(blocks, axis=0)

    def prep(pi, q):
        r0 = pi * P
        sl = slice(q * QUAD, (q + 1) * QUAD)
        cols = lambda base: slice(base + q * QUAD, base + (q + 1) * QUAD)
        load = lambda c: zs_ref[r0:r0 + P, c].astype(F32)
        vq = lambda k: vecs_ref[k:k + 1, sl]
        xr, xk, xv = load(cols(0)), load(cols(RWKV_WIDTH)), load(cols(2 * RWKV_WIDTH))
        xwa = load(slice(3 * RWKV_WIDTH, 3 * RWKV_WIDTH + LANES))
        xg = load(slice(3 * RWKV_WIDTH + LANES, RWKV_SHIFT_WIDTH))
        yield

        lin = jnp.where(low, jnp.tanh(xwa), xwa).astype(BF16)
        lora_w = _dot(lin, wa_ref[:, cols(0)])
        lora_a = _dot(lin, wa_ref[:, cols(RWKV_WIDTH)])
        gate_ref[r0:r0 + P, sl] = _dot(_sigmoid(xg).astype(BF16), g2_ref[:, sl])
        kk = xk * vq(V_KK)
        kk_norm2 = _dot((kk * kk).astype(BF16), ones_bd)
        yield

        lw = decay_scale * _sigmoid(vq(V_W0) + lora_w)
        a = _sigmoid(vq(V_A0) + lora_a)
        kk = kk / jnp.maximum(jnp.sqrt(kk_norm2), L2_EPS)
        k = xk * (1.0 + (a - 1.0) * vq(V_KA))
        yield

        bonus_ref[r0:r0 + P, sl] = _dot((xr * k * vq(V_RK)).astype(BF16), ones_bd) * xv
        d1 = _dot_split_rhs(emat, lw, 2)
        yield

        e_neg = jnp.exp(-d1)
        ops_ref[OP_R, r0:r0 + P, sl] = (xr * jnp.exp(d1)).astype(BF16)
        ops_ref[OP_A, r0:r0 + P, sl] = (-kk * jnp.exp(d1 - lw)).astype(BF16)
        yield

        ops_ref[OP_B, r0:r0 + P, sl] = (kk * a * e_neg).astype(BF16)
        ops_ref[OP_K, r0:r0 + P, sl] = (k * e_neg).astype(BF16)
        ops_ref[OP_V, r0:r0 + P, sl] = xv.astype(BF16)
        for j in range(P // C):
            ci = pi * (P // C) + j
            rho = lw[j * C:j * C + 1, :] - d1[j * C:j * C + 1, :]
            sin_ref[ci, :, sl] = jnp.exp(rho)
            sout_ref[ci, :, sl] = jnp.exp(d1[(j + 1) * C - 1:(j + 1) * C, :])

    def prep_work(pi):
        items = []
        for q in range(N_QUADS):
            gen = prep(pi, q)
            items += [lambda gen=gen: next(gen, None)] * PREP_STAGES
        return items

    def chains(grp, hooks):
        chains = []
        for ci in range(grp[0] // C, grp[1] // C):
            rows = slice(ci * C, (ci + 1) * C)
            for q in range(N_QUADS):
                sl = slice(q * QUAD, (q + 1) * QUAD)
                chains.append(dict(ci=ci, rows=rows, q=q, sl=sl,
                                   a=ops_ref[OP_A, rows, sl], r=ops_ref[OP_R, rows, sl], b=ops_ref[OP_B, rows, sl],
                                   k=ops_ref[OP_K, rows, sl], v=ops_ref[OP_V, rows, sl]))
        step = 0

        def run_hook():
            nonlocal step
            for item in hooks.get(step, []):
                item()
            step += 1

        for ch in chains:
            lg = jnp.concatenate([ch["a"], ch["r"]], axis=0)
            mf = _dot_nt(lg, jnp.concatenate([stack4(ch["b"]), stack4(ch["k"])], axis=0))
            ch["n"] = jnp.where(rs_strict, mf[:C, :QUAD], 0.0)
            ch["a_ak"] = jnp.where(rs_strict, mf[:C, QUAD:], 0.0).astype(BF16)
            ch["a_rb"] = jnp.where(rs_incl, mf[C:, :QUAD], 0.0).astype(BF16)
            ch["a_rk"] = jnp.where(rs_incl, mf[C:, QUAD:], 0.0).astype(BF16)
            ch["t"] = eye_rs + ch["n"]
            run_hook()
        for ch in chains:
            n_b = ch["n"].astype(BF16)
            ch["p"] = _dot(n_b, stack4(n_b)).astype(BF16)
            run_hook()
        pw = 2
        while pw < C:
            last = 2 * pw >= C
            for ch in chains:
                t_b = ch["t"].astype(BF16)
                lhs = t_b if last else jnp.concatenate([ch["p"], t_b], axis=0)
                res = _dot(lhs, stack4(ch["p"]))
                if last:
                    ch["t"] = ch["t"] + res
                else:
                    ch["p"] = res[:C].astype(BF16)
                    ch["t"] = ch["t"] + res[C:]
                run_hook()
            pw *= 2
        for ch in chains:
            akv = _dot(jnp.concatenate([ch["a_ak"], ch["a_rk"]], axis=0), stack4(ch["v"]))
            t_b = ch["t"].astype(BF16)
            art = _dot(ch["a_rb"], stack4(t_b)).astype(BF16)
            wu = _dot(jnp.concatenate([t_b, art], axis=0),
                      jnp.concatenate([stack4(ch["a"]), stack4(akv[:C].astype(BF16))], axis=1))
            ch["w"] = wu[:C, :QUAD]
            ch["u0"] = wu[:C, QUAD:].astype(BF16)
            ch["rw"] = ch["r"].astype(F32) + wu[C:, :QUAD]
            yc_ref[ch["rows"], ch["sl"]] = wu[C:, QUAD:] + akv[C:]
            run_hook()
        for ch in chains:
            s_in = sin_ref[ch["ci"]][:, ch["sl"]]
            s_out = sout_ref[ch["ci"]][:, ch["sl"]]
            rw_ref[ch["rows"], ch["sl"]] = (ch["rw"] * s_in).astype(BF16)
            w_in = (ch["w"] * s_in).astype(BF16)
            b_out = (ch["b"].astype(F32) * s_out).astype(BF16)
            k_out = (ch["k"].astype(F32) * s_out).astype(BF16)
            d_row = jnp.broadcast_to(s_in * s_out, (PAIR, QUAD))
            for h2 in range(QUAD // PAIR):
                ps = slice(h2 * PAIR, (h2 + 1) * PAIR)
                p = ch["q"] * (QUAD // PAIR) + h2
                bk = jnp.concatenate([b_out[:, ps], k_out[:, ps]], axis=0)
                wuv = jnp.concatenate(
                    [jnp.concatenate([w_in[:, ps], ch["u0"][:, ps]], axis=1),
                     jnp.concatenate([jnp.zeros_like(w_in[:, ps]), ch["v"][:, ps]], axis=1)], axis=0)
                pd = _dot_tn(bk, wuv)
                phi_ref[ch["ci"], p] = jnp.where(same_head, pd[:, :PAIR], 0.0).astype(BF16)
                delta_ref[ch["ci"], p] = jnp.where(same_head, pd[:, PAIR:], 0.0)
                dcol_ref[ch["ci"], p] = jnp.sum(jnp.where(eye_p, d_row[:, ps], 0.0), axis=1, keepdims=True)

    def state_step(c):
        rows = slice(c * C, (c + 1) * C)
        for p in range(N_PAIRS):
            sl = slice(p * PAIR, (p + 1) * PAIR)
            h = state_ref[p]
            res = _dot(jnp.concatenate([rw_ref[rows, sl], phi_ref[c, p]], axis=0), h.astype(BF16))
            y_ref[rows, sl] = res[:C] + yc_ref[rows, sl]
            state_ref[p] = dcol_ref[c, p] * h + res[C:] + delta_ref[c, p]

    def finish(grp):
        blk = slice(grp[0], grp[1])
        y = y_ref[blk, :]
        m1 = head_sum(y) * (1.0 / HEAD_DIM)
        dy = y - m1
        var = head_sum(dy * dy) * (1.0 / HEAD_DIM)
        yn = dy * lax.rsqrt(var + GN_EPS) * vec(V_LNW) + vec(V_LNB)
        o_ref[blk, :] = ((yn + bonus_ref[blk, :]) * gate_ref[blk, :]).astype(BF16)

    def tail_work(grp):
        return [lambda c=c: state_step(c) for c in range(grp[0] // C, grp[1] // C)] + [lambda: finish(grp)]

    bounds = list(range(0, TM_RWKV + 1, G))
    groups = list(zip(bounds[:-1], bounds[1:]))

    def prep_items(grp):
        return [item for pi in range(grp[0] // P, grp[1] // P) for item in prep_work(pi)]

    for item in prep_items(groups[0]):
        item()
    for gi, grp in enumerate(groups):
        work = []
        if gi + 1 < len(groups):
            work += prep_items(groups[gi + 1])
        if gi > 0:
            work += tail_work(groups[gi - 1])
        hooks = {}
        n_hooks = N_HOOKS_PER_CHAIN * (grp[1] - grp[0]) // C * N_QUADS
        for i, item in enumerate(work):
            hooks.setdefault(i * n_hooks // len(work), []).append(item)
        chains(grp, hooks)
    for item in tail_work(groups[-1]):
        item()


def _rwkv(zs3, vecs, wa_bd, g2, ones_bd):
    b, t, _ = zs3.shape
    const = lambda shape: pl.BlockSpec(shape, lambda bi, i: (0,) * len(shape))
    return pl.pallas_call(
        _rwkv_kernel,
        grid=(b, t // TM_RWKV),
        in_specs=[
            pl.BlockSpec((None, TM_RWKV, RWKV_SHIFT_WIDTH), lambda bi, i: (bi, i, 0)),
            const((N_VECS, RWKV_WIDTH)),
            const((LANES, 2 * RWKV_WIDTH)),
            const((GATE_LORA, RWKV_WIDTH)),
            const((MXU_WIDTH, MXU_WIDTH)),
        ],
        out_specs=pl.BlockSpec((None, TM_RWKV, RWKV_WIDTH), lambda bi, i: (bi, i, 0)),
        out_shape=jax.ShapeDtypeStruct((b, t, RWKV_WIDTH), BF16),
        scratch_shapes=[
            pltpu.VMEM((N_PAIRS, PAIR, PAIR), F32),
            pltpu.VMEM((N_CHUNKS, N_PAIRS, PAIR, PAIR), BF16),
            pltpu.VMEM((N_CHUNKS, N_PAIRS, PAIR, PAIR), F32),
            pltpu.VMEM((TM_RWKV, RWKV_WIDTH), BF16),
            pltpu.VMEM((TM_RWKV, RWKV_WIDTH), F32),
            pltpu.VMEM((N_CHUNKS, 1, RWKV_WIDTH), F32),
            pltpu.VMEM((N_CHUNKS, 1, RWKV_WIDTH), F32),
            pltpu.VMEM((TM_RWKV, RWKV_WIDTH), F32),
            pltpu.VMEM((TM_RWKV, RWKV_WIDTH), F32),
            pltpu.VMEM((TM_RWKV, RWKV_WIDTH), F32),
            pltpu.VMEM((5, TM_RWKV, RWKV_WIDTH), BF16),
            pltpu.VMEM((N_CHUNKS, N_PAIRS, PAIR, 1), F32),
        ],
        compiler_params=pltpu.CompilerParams(
            dimension_semantics=("arbitrary", "arbitrary"), vmem_limit_bytes=VMEM_LIMIT),
        name="rwkv",
    )(zs3, vecs, wa_bd, g2, ones_bd)


def _mlp_kernel(x_ref, at_ref, rw_ref, wo_ref, g1_ref, wu_ref, wd_ref, g2_ref, o_ref, act_ref):
    half = TM_MLP // 2
    rows_a, rows_b = slice(0, half), slice(half, TM_MLP)
    n_f = D_FF // TF_MLP

    def out_proj(rows):
        return (x_ref[rows, :] + _dot(at_ref[rows, :], wo_ref[:ATTN_WIDTH, :])
                + _dot(rw_ref[rows, :], wo_ref[ATTN_WIDTH:, :]))

    def up_piece(rows, h, f):
        sl = slice(f * TF_MLP, (f + 1) * TF_MLP)
        up = jnp.maximum(_dot(h, wu_ref[:, sl]), 0.0)
        act_ref[rows, sl] = (up * up).astype(BF16)

    x1a = out_proj(rows_a)
    x1b = out_proj(rows_b)
    ha = _rms_norm(x1a, g1_ref[...]).astype(BF16)
    up_piece(rows_a, ha, 0)
    hb = _rms_norm(x1b, g1_ref[...]).astype(BF16)
    for f in range(1, n_f):
        up_piece(rows_a, ha, f)
    x2a = x1a + _dot(act_ref[rows_a, :], wd_ref[...])
    up_piece(rows_b, hb, 0)
    o_ref[rows_a, :] = _rms_norm(x2a, g2_ref[...])
    for f in range(1, n_f):
        up_piece(rows_b, hb, f)
    x2b = x1b + _dot(act_ref[rows_b, :], wd_ref[...])
    o_ref[rows_b, :] = _rms_norm(x2b, g2_ref[...])


def _mlp(x2, attn, rwkv, w_out, g_mlp, w_up, w_down, g_final):
    n = x2.shape[0]
    const = lambda shape: pl.BlockSpec(shape, lambda i: (0, 0), pipeline_mode=pl.Buffered(1))
    return pl.pallas_call(
        _mlp_kernel,
        grid=(n // TM_MLP,),
        in_specs=[
            pl.BlockSpec((TM_MLP, D_MODEL), lambda i: (i, 0)),
            pl.BlockSpec((TM_MLP, ATTN_WIDTH), lambda i: (i, 0)),
            pl.BlockSpec((TM_MLP, RWKV_WIDTH), lambda i: (i, 0)),
            const((D_MODEL, D_MODEL)),
            const((1, D_MODEL)),
            const((D_MODEL, D_FF)),
            const((D_FF, D_MODEL)),
            const((1, D_MODEL)),
        ],
        out_specs=pl.BlockSpec((TM_MLP, D_MODEL), lambda i: (i, 0)),
        out_shape=jax.ShapeDtypeStruct((n, D_MODEL), F32),
        scratch_shapes=[pltpu.VMEM((TM_MLP, D_FF), BF16)],
        compiler_params=pltpu.CompilerParams(
            dimension_semantics=("arbitrary",), vmem_limit_bytes=VMEM_LIMIT),
        name="mlp",
    )(x2, attn, rwkv, w_out, g_mlp, w_up, w_down, g_final)


def _layer(x, attn_norm_g, w_in, attn_sinks, rwkv_mu, w0, w2, a0, a2, g2, k_k, k_a, r_k,
           ln_x_w, ln_x_b, w_out, mlp_norm_g, w_up, w_down):
    b, t, d = x.shape
    x2 = x.reshape(b * t, d)
    row = lambda v: v.reshape(1, -1)

    assert t % TM_PROJ == 0 and t % TM_RWKV == 0
    zs, attn = _proj_attn(x2, row(attn_norm_g), w_in.astype(BF16), row(rwkv_mu), attn_sinks, t // TM_PROJ)

    zero = jnp.zeros((DECAY_LORA, RWKV_WIDTH), F32)
    wa_bd = jnp.concatenate(
        [jnp.concatenate([w2, zero], axis=1), jnp.concatenate([zero, a2], axis=1)], axis=0).astype(BF16)
    vecs = jnp.stack([w0, a0, k_k, k_a, r_k, ln_x_w, ln_x_b, jnp.zeros_like(w0)], axis=0)
    head_id = jnp.arange(MXU_WIDTH) // HEAD_DIM
    ones_bd = (head_id[:, None] == head_id[None, :]).astype(BF16)
    rwkv = _rwkv(zs.reshape(b, t, RWKV_SHIFT_WIDTH), vecs, wa_bd, g2.astype(BF16), ones_bd)

    return x2, attn.reshape(b * t, ATTN_WIDTH), rwkv.reshape(b * t, RWKV_WIDTH)


def kernel(x, attn_norm_g, w_in, attn_sinks, rwkv_mu, w0, w2, a0, a2, g2, k_k, k_a, r_k, ln_x_w, ln_x_b,
           w_out, mlp_norm_g, w_up, w_down, final_norm_g):
    assert attn_norm_g.shape[0] == 1, "one trunk layer"
    b, t, d = x.shape
    x2, attn, rwkv = _layer(x, attn_norm_g[0], w_in[0], attn_sinks[0], rwkv_mu[0], w0[0], w2[0], a0[0], a2[0],
                            g2[0], k_k[0], k_a[0], r_k[0], ln_x_w[0], ln_x_b[0], w_out[0], mlp_norm_g[0],
                            w_up[0], w_down[0])
    out = _mlp(x2, attn, rwkv, w_out[0].astype(BF16), mlp_norm_g[0].reshape(1, -1), w_up[0].astype(BF16),
               w_down[0].astype(BF16), final_norm_g.reshape(1, -1))
    return out.reshape(b, t, d)
```
